```python
import jax, jax.numpy as jnp
from jax import lax
import numpy as np

D_MODEL = 1024
BATCH = 8
SEQ = 4096
DEPTH = 1

A_HEADS = 8
A_HEAD_DIM = 64
A_WIDTH = A_HEADS * A_HEAD_DIM
MOBA_BLOCK = 256
MOBA_TOPK = 3
MOBA_QCHUNK = 32
ROPE_THETA = 10000.0
B_HEADS = 4
B_HEAD_DIM = 128
B_WIDTH = B_HEADS * B_HEAD_DIM
MLSTM_CHUNK = 64
CONV_WIDTH = 4
D_FF = ((8 * D_MODEL + 3 * 256 - 1) // (3 * 256)) * 256
NORM_EPS = 1e-6

IN_SIZES = [A_WIDTH, A_WIDTH, A_WIDTH,
            2 * B_WIDTH,
            B_WIDTH, B_WIDTH,
            B_HEADS, B_HEADS,
            D_MODEL, D_MODEL]
IN_WIDTH = sum(IN_SIZES)
IN_SPLITS = [int(s) for s in np.cumsum(IN_SIZES)[:-1]]

kernel_name = "moba_mlstm_gated_hybrid"


def rms_norm(x, g):
    xf = x.astype(jnp.float32)
    y = xf * lax.rsqrt(jnp.mean(xf * xf, axis=-1, keepdims=True) + NORM_EPS)
    return (y * g.astype(jnp.float32)).astype(x.dtype)


def rope(x, pos):
    half = x.shape[-1] // 2
    inv = ROPE_THETA ** (-jnp.arange(half, dtype=jnp.float32) / half)
    ang = pos.astype(jnp.float32)[:, None] * inv[None, :]
    cos, sin = jnp.cos(ang), jnp.sin(ang)
    xf = x.astype(jnp.float32)
    x1, x2 = xf[..., :half], xf[..., half:]
    return jnp.concatenate([x1 * cos - x2 * sin, x2 * cos + x1 * sin], axis=-1).astype(x.dtype)


def causal_depthwise_conv(x, w):
    width, chans = w.shape
    return lax.conv_general_dilated(
        x, w[:, None, :].astype(x.dtype), window_strides=(1,),
        padding=((width - 1, 0),), dimension_numbers=('NWC', 'WIO', 'NWC'),
        feature_group_count=chans)


def moba_attention(q, k, v):
    bn, nh, s, dh = q.shape
    nb = -(-s // MOBA_BLOCK)
    sp = nb * MOBA_BLOCK
    pad = ((0, 0), (0, 0), (0, sp - s), (0, 0))
    q, k, v = jnp.pad(q, pad), jnp.pad(k, pad), jnp.pad(v, pad)
    kb = k.reshape(bn, nh, nb, MOBA_BLOCK, dh)
    vb = v.reshape(bn, nh, nb, MOBA_BLOCK, dh)
    scale = dh ** -0.5
    topk = min(MOBA_TOPK, nb - 1)
    nq = sp // MOBA_QCHUNK
    qc = jnp.moveaxis(q.reshape(bn, nh, nq, MOBA_QCHUNK, dh), 2, 0)
    starts = jnp.arange(nq) * MOBA_QCHUNK
    if topk > 0:
        kmean = kb.astype(jnp.float32).mean(axis=3)
        gate = jnp.einsum('bhsd,bhnd->bhsn', q.astype(jnp.float32), kmean)
        q_blk = jnp.arange(sp) // MOBA_BLOCK
        past = jnp.arange(nb)[None, :] < q_blk[:, None]
        gate = jnp.where(past, gate, -jnp.inf)
        top_val, top_idx = lax.top_k(gate, topk)
        valid = jnp.isfinite(top_val)
        idx_c = jnp.moveaxis(top_idx.reshape(bn, nh, nq, MOBA_QCHUNK, topk), 2, 0)
        val_c = jnp.moveaxis(valid.reshape(bn, nh, nq, MOBA_QCHUNK, topk), 2, 0)
        xs = (qc, starts, idx_c, val_c)
    else:
        xs = (qc, starts)
    bi = jnp.arange(bn)[:, None, None, None]
    hi = jnp.arange(nh)[None, :, None, None]

    def attend_chunk(args):
        qi, start = args[0], args[1]
        blk = start // MOBA_BLOCK
        k_own = lax.dynamic_index_in_dim(kb, blk, axis=2, keepdims=False)
        v_own = lax.dynamic_index_in_dim(vb, blk, axis=2, keepdims=False)
        qpos = start + jnp.arange(MOBA_QCHUNK)
        kpos = blk * MOBA_BLOCK + jnp.arange(MOBA_BLOCK)
        causal = kpos[None, :] <= qpos[:, None]
        s_own = jnp.einsum('bhqd,bhkd->bhqk', qi, k_own).astype(jnp.float32) * scale
        s_own = jnp.where(causal, s_own, -jnp.inf)
        if topk > 0:
            idx_i, valid_i = args[2], args[3]
            k_sel = kb[bi, hi, idx_i]
            v_sel = vb[bi, hi, idx_i]
            s_sel = jnp.einsum('bhqd,bhqrkd->bhqrk', qi, k_sel).astype(jnp.float32) * scale
            s_sel = jnp.where(valid_i[..., None], s_sel, -jnp.inf)
            s_sel = s_sel.reshape(bn, nh, MOBA_QCHUNK, topk * MOBA_BLOCK)
            p = jax.nn.softmax(jnp.concatenate([s_sel, s_own], axis=-1), axis=-1).astype(v.dtype)
            p_sel = p[..., :topk * MOBA_BLOCK].reshape(bn, nh, MOBA_QCHUNK, topk, MOBA_BLOCK)
            return (jnp.einsum('bhqrk,bhqrkd->bhqd', p_sel, v_sel)
                    + jnp.einsum('bhqk,bhkd->bhqd', p[..., topk * MOBA_BLOCK:], v_own))
        p = jax.nn.softmax(s_own, axis=-1).astype(v.dtype)
        return jnp.einsum('bhqk,bhkd->bhqd', p, v_own)

    out = lax.map(attend_chunk, xs)
    return jnp.moveaxis(out, 0, 2).reshape(bn, nh, sp, dh)[:, :, :s]


def mlstm_chunkwise(q, k, v, i_pre, f_pre):
    bn, nh, s, d = q.shape
    L = MLSTM_CHUNK
    nc = s // L
    f32 = jnp.float32
    q = q.astype(f32)
    k = k.astype(f32) * (d ** -0.5)
    v = v.astype(f32)
    log_f = jax.nn.log_sigmoid(f_pre.astype(f32))
    log_i = i_pre.astype(f32)

    def chunks(a):
        return jnp.moveaxis(a.reshape(bn, nh, nc, L, *a.shape[3:]), 2, 0)

    tri = jnp.tril(jnp.ones((L, L), dtype=bool))

    def step(carry, xs):
        C, n, m = carry
        qc, kc, vc, ic, fc = xs
        b = jnp.cumsum(fc, axis=-1)
        dmat = jnp.where(tri, b[..., :, None] - b[..., None, :] + ic[..., None, :], -jnp.inf)
        m_inter = b + m[..., None]
        m_t = jnp.maximum(m_inter, dmat.max(axis=-1))
        sc = jnp.einsum('bhtd,bhsd->bhts', qc, kc) * jnp.exp(dmat - m_t[..., None])
        w_inter = jnp.exp(m_inter - m_t)
        num = (jnp.einsum('bhts,bhse->bhte', sc, vc)
               + w_inter[..., None] * jnp.einsum('bhtd,bhde->bhte', qc, C))
        den = sc.sum(axis=-1) + w_inter * jnp.einsum('bhtd,bhd->bht', qc, n)
        h = num / jnp.maximum(jnp.abs(den), jnp.exp(-m_t))[..., None]
        b_tot = b[..., -1]
        g = b_tot[..., None] - b + ic
        m_new = jnp.maximum(b_tot + m, g.max(axis=-1))
        w_c = jnp.exp(b_tot + m - m_new)
        w_s = jnp.exp(g - m_new[..., None])
        C = w_c[..., None, None] * C + jnp.einsum('bhs,bhsd,bhse->bhde', w_s, kc, vc)
        n = w_c[..., None] * n + jnp.einsum('bhs,bhsd->bhd', w_s, kc)
        return (C, n, m_new), h

    init = (jnp.zeros((bn, nh, d, d), f32), jnp.zeros((bn, nh, d), f32), jnp.zeros((bn, nh), f32))
    _, hs = lax.scan(step, init, (chunks(q), chunks(k), chunks(v), chunks(log_i), chunks(log_f)))
    return jnp.moveaxis(hs, 0, 2).reshape(bn, nh, s, d)


def setup_inputs(seed: int = 0) -> dict:
    key = jax.random.key(seed)
    ks = jax.random.split(key, 16)
    nrm = jax.random.normal
    f32 = jnp.float32
    return {
        "x": nrm(ks[0], (BATCH, SEQ, D_MODEL), f32),
        "norm_mix_g": 1.0 + 0.02 * nrm(ks[1], (DEPTH, D_MODEL), f32),
        "w_in": nrm(ks[2], (DEPTH, D_MODEL, IN_WIDTH), f32) * D_MODEL ** -0.5,
        "conv_w": nrm(ks[3], (DEPTH, CONV_WIDTH, 2 * B_WIDTH), f32) * CONV_WIDTH ** -0.5,
        "b_igate": 0.1 * nrm(ks[4], (DEPTH, B_HEADS), f32),
        "b_fgate": jnp.linspace(3.0, 6.0, B_HEADS, dtype=f32)[None, :] + 0.1 * nrm(ks[5], (DEPTH, B_HEADS), f32),
        "mlstm_norm_g": 1.0 + 0.02 * nrm(ks[6], (DEPTH, B_WIDTH), f32),
        "w_proj_a": nrm(ks[7], (DEPTH, A_WIDTH, D_MODEL), f32) * A_WIDTH ** -0.5,
        "w_proj_b": nrm(ks[8], (DEPTH, B_WIDTH, D_MODEL), f32) * B_WIDTH ** -0.5,
        "w_out": nrm(ks[9], (DEPTH, D_MODEL, D_MODEL), f32) * D_MODEL ** -0.5,
        "norm_ffn_g": 1.0 + 0.02 * nrm(ks[10], (DEPTH, D_MODEL), f32),
        "w_gate_up": nrm(ks[11], (DEPTH, D_MODEL, 2 * D_FF), f32) * D_MODEL ** -0.5,
        "w_down": nrm(ks[12], (DEPTH, D_FF, D_MODEL), f32) * D_FF ** -0.5,
        "norm_final_g": 1.0 + 0.02 * nrm(ks[13], (D_MODEL,), f32),
    }


def reference(x, norm_mix_g, w_in, conv_w, b_igate, b_fgate, mlstm_norm_g, w_proj_a, w_proj_b,
              w_out, norm_ffn_g, w_gate_up, w_down, norm_final_g):
    bn, s, _ = x.shape
    pos = jnp.arange(s)

    def to_heads(t, nh):
        return t.reshape(bn, s, nh, -1).transpose(0, 2, 1, 3)

    h = x
    for l in range(DEPTH):
        u = rms_norm(h, norm_mix_g[l])
        z = u @ w_in[l]
        qa, ka, va, qk_b, vb, ob, ib, fb, ga, gb = jnp.split(z, IN_SPLITS, axis=-1)

        qa = rope(to_heads(qa, A_HEADS), pos)
        ka = rope(to_heads(ka, A_HEADS), pos)
        ya = moba_attention(qa, ka, to_heads(va, A_HEADS))
        ya = ya.transpose(0, 2, 1, 3).reshape(bn, s, A_WIDTH)

        qk_b = jax.nn.silu(causal_depthwise_conv(qk_b, conv_w[l]))
        qb, kb_ = jnp.split(qk_b, 2, axis=-1)
        i_pre = (ib + b_igate[l].astype(ib.dtype)).transpose(0, 2, 1)
        f_pre = (fb + b_fgate[l].astype(fb.dtype)).transpose(0, 2, 1)
        hb = mlstm_chunkwise(to_heads(qb, B_HEADS), to_heads(kb_, B_HEADS), to_heads(vb, B_HEADS),
                             i_pre, f_pre)
        hb = hb * lax.rsqrt(jnp.mean(hb * hb, axis=-1, keepdims=True) + NORM_EPS)
        hb = hb.transpose(0, 2, 1, 3).reshape(bn, s, B_WIDTH) * mlstm_norm_g[l].astype(jnp.float32)
        yb = (jax.nn.sigmoid(ob.astype(jnp.float32)) * hb).astype(u.dtype)

        merged = (jax.nn.sigmoid(ga) * (ya @ w_proj_a[l])
                  + jax.nn.sigmoid(gb) * (yb @ w_proj_b[l]))
        h = h + merged @ w_out[l]

        u = rms_norm(h, norm_ffn_g[l])
        g, up = jnp.split(u @ w_gate_up[l], 2, axis=-1)
        h = h + (jax.nn.silu(g) * up) @ w_down[l]

    return rms_norm(h, norm_final_g)
```

```python
import functools

import jax
import jax.numpy as jnp
import numpy as np
from jax import lax
from jax.experimental import pallas as pl
from jax.experimental.pallas import tpu as pltpu

F32 = jnp.float32
BF16 = jnp.bfloat16

D_MODEL = 1024
A_HEADS = 8
A_HEAD_DIM = 64
A_WIDTH = A_HEADS * A_HEAD_DIM
MOBA_BLOCK = 256
MOBA_TOPK = 3
ROPE_THETA = 10000.0
B_HEADS = 4
B_HEAD_DIM = 128
B_WIDTH = B_HEADS * B_HEAD_DIM
CONV_WIDTH = 4
D_FF = 2816
NORM_EPS = 1e-6

LANES = 128
SUBLANES = 8
BF16_SUBLANE_PACK = 16
VMEM_LIMIT_BYTES = 56 * 1024 * 1024

TM_IN = 512
MLSTM_L = 256
TS_MLSTM = 1024
TM_OUT = 512
FF_CHUNK = 256
AUG_ROWS = B_HEAD_DIM + BF16_SUBLANE_PACK


def _dot(a, b):
    return jnp.dot(a, b, preferred_element_type=F32)


def _dot_nt(a, b):
    return lax.dot_general(a, b, (((1,), (1,)), ((), ())), preferred_element_type=F32)


def _split3(x):
    x1 = x.astype(BF16)
    r1 = x - x1.astype(F32)
    x2 = r1.astype(BF16)
    r2 = r1 - x2.astype(F32)
    return x1, x2, r2.astype(BF16)


def _const_spec(shape):
    nd = len(shape)
    return pl.BlockSpec(shape, lambda *_: (0,) * nd, pipeline_mode=pl.Buffered(1))


def _rope_table_kernel(inv_ref, cos_ref, sin_ref):
    s = cos_ref.shape[0]
    pos = lax.broadcasted_iota(jnp.int32, (s, LANES), 0).astype(F32)
    lane = lax.broadcasted_iota(jnp.int32, (s, LANES), 1)
    ang = pos * inv_ref[...]
    first_half = (lane % A_HEAD_DIM) < (A_HEAD_DIM // 2)
    cos_ref[...] = jnp.cos(ang)
    sn = jnp.sin(ang)
    sin_ref[...] = jnp.where(first_half, -sn, sn)


def _rope_tables(s):
    half = A_HEAD_DIM // 2
    inv = ROPE_THETA ** (-jnp.arange(half, dtype=F32) / half)
    inv = jnp.tile(inv, LANES // half)[None, :]
    return pl.pallas_call(
        _rope_table_kernel,
        out_shape=(jax.ShapeDtypeStruct((s, LANES), F32), jax.ShapeDtypeStruct((s, LANES), F32)),
        name="rope_tables",
    )(inv)


def _log_sigmoid(x):
    return jnp.minimum(x, 0.0) - jnp.log(1.0 + jnp.exp(-jnp.abs(x)))


def _inproj_kernel(x_ref, g_ref, cos_ref, sin_ref, convw_ref, bcol_ref, brow_ref,
                   wqka_ref, wqkb_ref, wo_ref, wg_ref, wif_ref, wvt_ref, wift_ref,
                   qa_ref, ka_ref, kmean_ref, vat_ref, vbt_ref, qb_ref, kb_ref, ob_ref,
                   sga_ref, sgb_ref, grow_ref, gcol_ref,
                   u_ref, carry_ref):
    tm = x_ref.shape[1]
    nchunk = tm // MOBA_BLOCK
    t_idx = pl.program_id(1)

    xf = x_ref[0]
    var = jnp.mean(xf * xf, axis=-1, keepdims=True)
    u_ref[...] = (xf * lax.rsqrt(var + NORM_EPS) * g_ref[...]).astype(BF16)
    u = u_ref[...]

    cos = jnp.tile(cos_ref[...], (1, A_WIDTH // LANES))
    sin = jnp.tile(sin_ref[...], (1, A_WIDTH // LANES))
    lane = lax.broadcasted_iota(jnp.int32, (tm, A_WIDTH), 1)
    first_half = (lane % A_HEAD_DIM) < (A_HEAD_DIM // 2)

    def rope(z):
        partner = jnp.where(first_half,
                            pltpu.roll(z, A_WIDTH - A_HEAD_DIM // 2, 1),
                            pltpu.roll(z, A_HEAD_DIM // 2, 1))
        return z * cos + partner * sin

    q = rope(_dot(u, wqka_ref[:, :A_WIDTH]))
    qa_ref[0] = (q * (A_HEAD_DIM ** -0.5)).astype(BF16)
    k = rope(_dot(u, wqka_ref[:, A_WIDTH:]))
    ka_ref[0] = k.astype(BF16)
    for c in range(nchunk):
        kmean_ref[0, 0, c:c + 1, :] = jnp.mean(k[c * MOBA_BLOCK:(c + 1) * MOBA_BLOCK], axis=0, keepdims=True)

    vat = _dot_nt(wvt_ref[:A_WIDTH, :], u).astype(BF16)
    vbt = _dot_nt(wvt_ref[A_WIDTH:, :], u).astype(BF16)
    for c in range(nchunk):
        vat_ref[0, c] = vat[:, c * MOBA_BLOCK:(c + 1) * MOBA_BLOCK]
        vbt_ref[0, c] = vbt[:, c * MOBA_BLOCK:(c + 1) * MOBA_BLOCK]

    @pl.when(t_idx == 0)
    def _():
        carry_ref[...] = jnp.zeros_like(carry_ref)

    zqk = _dot(u, wqkb_ref[...])
    ext = jnp.concatenate([carry_ref[...], zqk], axis=0)
    carry_ref[...] = zqk[tm - SUBLANES:, :]
    conv = None
    for j in range(CONV_WIDTH):
        off = SUBLANES - (CONV_WIDTH - 1) + j
        term = ext[off:off + tm, :] * convw_ref[j:j + 1, :]
        conv = term if conv is None else conv + term
    act = conv * jax.nn.sigmoid(conv)
    qb_ref[0] = act[:, :B_WIDTH].astype(BF16)
    kb_ref[0] = (act[:, B_WIDTH:] * (B_HEAD_DIM ** -0.5)).astype(BF16)

    ob_ref[0] = _dot(u, wo_ref[...]).astype(BF16)

    sga_ref[0] = jax.nn.sigmoid(_dot(u, wg_ref[:, :D_MODEL])).astype(BF16)
    sgb_ref[0] = jax.nn.sigmoid(_dot(u, wg_ref[:, D_MODEL:])).astype(BF16)

    L = MLSTM_L
    zc = _dot(u, wif_ref[...]) + bcol_ref[...]
    lane8 = lax.broadcasted_iota(jnp.int32, zc.shape, 1)
    pre_c = jnp.where(lane8 < B_HEADS, _log_sigmoid(zc), zc)
    zr = _dot_nt(wift_ref[...], u) + brow_ref[...]
    row8 = lax.broadcasted_iota(jnp.int32, zr.shape, 0)
    pre_r = jnp.where(row8 < B_HEADS, _log_sigmoid(zr), zr)
    ti = lax.broadcasted_iota(jnp.int32, (L, L), 0)
    tj = lax.broadcasted_iota(jnp.int32, (L, L), 1)
    tril = jnp.where(tj <= ti, 1.0, 0.0).astype(BF16)
    triu = jnp.where(ti <= tj, 1.0, 0.0).astype(BF16)
    lane8_c = lax.broadcasted_iota(jnp.int32, (L, 2 * B_HEADS), 1)
    row8_c = lax.broadcasted_iota(jnp.int32, (2 * B_HEADS, L), 0)
    for c in range(tm // L):
        pc = pre_c[c * L:(c + 1) * L, :]
        p1, p2, p3 = _split3(pc)
        cum = _dot(tril, p1) + _dot(tril, p2) + _dot(tril, p3)
        gcol_ref[0, c * L:(c + 1) * L, :] = jnp.where(lane8_c < B_HEADS, cum, pc)
        pr = pre_r[:, c * L:(c + 1) * L]
        r1, r2, r3 = _split3(pr)
        cumr = _dot(r1, triu) + _dot(r2, triu) + _dot(r3, triu)
        grow_ref[0, c] = jnp.where(row8_c < B_HEADS, cumr, pr)


def _in_proj(x, norm_g, cos_t, sin_t, conv_w, bcol, brow, wqka, wqkb, wo, wg, wif, wvt, wift):
    bn, s, d = x.shape
    tm = TM_IN
    nt = s // tm
    nc = tm // MOBA_BLOCK
    nblk = s // MOBA_BLOCK
    tok = lambda w: pl.BlockSpec((1, tm, w), lambda b, t: (b, t, 0))
    blk4 = lambda r: pl.BlockSpec((1, nc, r, MOBA_BLOCK), lambda b, t: (b, t, 0, 0))
    in_specs = [
        tok(d),
        _const_spec((1, d)),
        pl.BlockSpec((tm, LANES), lambda b, t: (t, 0)),
        pl.BlockSpec((tm, LANES), lambda b, t: (t, 0)),
        _const_spec(conv_w.shape), _const_spec(bcol.shape), _const_spec(brow.shape),
        _const_spec(wqka.shape), _const_spec(wqkb.shape), _const_spec(wo.shape),
        _const_spec(wg.shape), _const_spec(wif.shape), _const_spec(wvt.shape), _const_spec(wift.shape),
    ]
    out_shape = (
        jax.ShapeDtypeStruct((bn, s, A_WIDTH), BF16),
        jax.ShapeDtypeStruct((bn, s, A_WIDTH), BF16),
        jax.ShapeDtypeStruct((bn, nt, nc, A_WIDTH), F32),
        jax.ShapeDtypeStruct((bn, nblk, A_WIDTH, MOBA_BLOCK), BF16),
        jax.ShapeDtypeStruct((bn, nblk, B_WIDTH, MOBA_BLOCK), BF16),
        jax.ShapeDtypeStruct((bn, s, B_WIDTH), BF16),
        jax.ShapeDtypeStruct((bn, s, B_WIDTH), BF16),
        jax.ShapeDtypeStruct((bn, s, B_WIDTH), BF16),
        jax.ShapeDtypeStruct((bn, s, D_MODEL), BF16),
        jax.ShapeDtypeStruct((bn, s, D_MODEL), BF16),
        jax.ShapeDtypeStruct((bn, nblk, 2 * B_HEADS, MOBA_BLOCK), F32),
        jax.ShapeDtypeStruct((bn, s, 2 * B_HEADS), F32),
    )
    out_specs = (
        tok(A_WIDTH), tok(A_WIDTH),
        pl.BlockSpec((1, 1, nc, A_WIDTH), lambda b, t: (b, t, 0, 0)),
        blk4(A_WIDTH), blk4(B_WIDTH),
        tok(B_WIDTH), tok(B_WIDTH), tok(B_WIDTH),
        tok(D_MODEL), tok(D_MODEL),
        blk4(2 * B_HEADS),
        tok(2 * B_HEADS),
    )
    return pl.pallas_call(
        _inproj_kernel,
        grid=(bn, nt),
        in_specs=in_specs,
        out_specs=out_specs,
        out_shape=out_shape,
        scratch_shapes=[pltpu.VMEM((tm, d), BF16), pltpu.VMEM((SUBLANES, 2 * B_WIDTH), F32)],
        compiler_params=pltpu.CompilerParams(
            dimension_semantics=("arbitrary", "arbitrary"), vmem_limit_bytes=VMEM_LIMIT_BYTES),
        name="in_proj",
    )(x, norm_g, cos_t, sin_t, conv_w, bcol, brow, wqka, wqkb, wo, wg, wif, wvt, wift)


def _moba_kernel(q_ref, k_ref, vt_ref, kmean_ref, o_ref, sel_ref):
    blk = MOBA_BLOCK
    nblk = kmean_ref.shape[1]
    heads = LANES // A_HEAD_DIM
    qi = pl.program_id(2)

    q = q_ref[0]
    lane = lax.broadcasted_iota(jnp.int32, q.shape, 1)
    qh = [jnp.where((lane >= h * A_HEAD_DIM) & (lane < (h + 1) * A_HEAD_DIM), q, jnp.zeros_like(q))
          for h in range(heads)]

    km = kmean_ref[0]
    km_hi = km.astype(BF16)
    km_lo = (km - km_hi.astype(F32)).astype(BF16)
    bidx = lax.broadcasted_iota(jnp.int32, (nblk, blk), 0)
    past = bidx < qi
    for h in range(heads):
        gate = _dot_nt(km_hi, qh[h]) + _dot_nt(km_lo, qh[h])
        gate = jnp.where(past, gate, -jnp.inf)
        rank = jnp.zeros((nblk, blk), F32)
        for jp in range(nblk):
            gj = gate[jp:jp + 1, :]
            beats = jnp.where(gj > gate, 1.0, jnp.where((gj == gate) & (bidx > jp), 1.0, 0.0))
            rank = rank + beats
        sel_ref[h] = jnp.where(past & (rank < MOBA_TOPK), 1.0, 0.0)

    key_i = lax.broadcasted_iota(jnp.int32, (blk, blk), 0)
    qry_i = lax.broadcasted_iota(jnp.int32, (blk, blk), 1)
    causal = key_i <= qry_i

    kd = k_ref[0, pl.ds(pl.multiple_of(qi * blk, blk), blk), :]
    state = []
    for h in range(heads):
        st = jnp.where(causal, _dot_nt(kd, qh[h]), -jnp.inf)
        m = jnp.max(st, axis=0, keepdims=True)
        p = jnp.exp(st - m)
        l = jnp.sum(p, axis=0, keepdims=True)
        acc = _dot(vt_ref[0, qi, h * A_HEAD_DIM:(h + 1) * A_HEAD_DIM, :], p.astype(BF16))
        state += [m, l, acc]

    def body(j, carry):
        kj = k_ref[0, pl.ds(pl.multiple_of(j * blk, blk), blk), :]
        out = []
        for h in range(heads):
            m, l, acc = carry[3 * h:3 * h + 3]
            selrow = sel_ref[h, pl.ds(j, 1), :]
            st = jnp.where(selrow > 0.5, _dot_nt(kj, qh[h]), -jnp.inf)
            mn = jnp.maximum(m, jnp.max(st, axis=0, keepdims=True))
            alpha = jnp.exp(m - mn)
            p = jnp.exp(st - mn)
            l = alpha * l + jnp.sum(p, axis=0, keepdims=True)
            acc = alpha * acc + _dot(vt_ref[0, j, h * A_HEAD_DIM:(h + 1) * A_HEAD_DIM, :], p.astype(BF16))
            out += [mn, l, acc]
        return tuple(out)

    state = lax.fori_loop(0, qi, body, tuple(state))
    ot = jnp.concatenate([state[3 * h + 2] / state[3 * h + 1] for h in range(heads)], axis=0)
    o_ref[0] = ot.T.astype(BF16)


def _moba(qa, ka, vat, kmean):
    bn, s, _ = qa.shape
    blk = MOBA_BLOCK
    nblk = s // blk
    npair = A_WIDTH // LANES
    return pl.pallas_call(
        _moba_kernel,
        grid=(bn, npair, nblk),
        in_specs=[
            pl.BlockSpec((1, blk, LANES), lambda b, p, i: (b, i, p)),
            pl.BlockSpec((1, s, LANES), lambda b, p, i: (b, 0, p)),
            pl.BlockSpec((1, nblk, LANES, blk), lambda b, p, i: (b, 0, p, 0)),
            pl.BlockSpec((1, nblk, LANES), lambda b, p, i: (b, 0, p)),
        ],
        out_specs=pl.BlockSpec((1, blk, LANES), lambda b, p, i: (b, i, p)),
        out_shape=jax.ShapeDtypeStruct((bn, s, A_WIDTH), BF16),
        scratch_shapes=[pltpu.VMEM((LANES // A_HEAD_DIM, nblk, blk), F32)],
        compiler_params=pltpu.CompilerParams(
            dimension_semantics=("arbitrary", "arbitrary", "arbitrary"), vmem_limit_bytes=VMEM_LIMIT_BYTES),
        name="moba_attention",
    )(qa, ka, vat, kmean)


def _mlstm_kernel(q_ref, k_ref, vt_ref, o_ref, grow_ref, gcol_ref, ng_ref, y_ref, ctn_ref, m_ref):
    L = MLSTM_L
    d = B_HEAD_DIM
    nchunk = q_ref.shape[1] // L

    @pl.when(pl.program_id(1) == 0)
    def _():
        ctn_ref[...] = jnp.zeros_like(ctn_ref)
        m_ref[...] = jnp.zeros_like(m_ref)

    s_i = lax.broadcasted_iota(jnp.int32, (L, L), 0)
    t_i = lax.broadcasted_iota(jnp.int32, (L, L), 1)
    causal = s_i <= t_i
    aug_row = lax.broadcasted_iota(jnp.int32, (BF16_SUBLANE_PACK, L), 0)
    ones_blk = jnp.where(aug_row == 0, 1.0, 0.0)

    def chunk(c, _):
        r0 = pl.multiple_of(c * L, L)
        grow = grow_ref[0, c]
        gcol = gcol_ref[0, pl.ds(r0, L), :]
        for h in range(B_HEADS):
            hs = slice(h * d, (h + 1) * d)
            qc = q_ref[0, pl.ds(r0, L), hs]
            kc = k_ref[0, pl.ds(r0, L), hs]
            vt = vt_ref[0, c, hs, :].astype(F32)
            b_row = grow[h:h + 1, :]
            i_row = grow[B_HEADS + h:B_HEADS + h + 1, :]
            r_col = gcol[:, B_HEADS + h:B_HEADS + h + 1] - gcol[:, h:h + 1]
            m_prev = m_ref[h, 0:1, :]

            dm = jnp.where(causal, r_col + b_row, -jnp.inf)
            m_inter = b_row + m_prev
            m_t = jnp.maximum(m_inter, jnp.max(dm, axis=0, keepdims=True))
            pw = (_dot_nt(kc, qc) * jnp.exp(dm - m_t)).astype(BF16)
            w_inter = jnp.exp(m_inter - m_t)
            vt_aug = jnp.concatenate([vt, ones_blk], axis=0).astype(BF16)
            ctn = ctn_ref[h]
            tot = _dot(vt_aug, pw) + w_inter * _dot_nt(ctn.astype(BF16), qc)
            den = tot[d:d + 1, :]
            ht = tot[:d, :] / jnp.maximum(jnp.abs(den), jnp.exp(-m_t))
            ms = jnp.mean(ht * ht, axis=0, keepdims=True)
            hn = (ht * lax.rsqrt(ms + NORM_EPS)).T
            og = jax.nn.sigmoid(o_ref[0, pl.ds(r0, L), hs].astype(F32))
            y_ref[0, pl.ds(r0, L), hs] = (og * (hn * ng_ref[:, hs])).astype(BF16)

            b_tot = b_row[:, L - 1:L]
            g = b_tot - b_row + i_row
            m_new = jnp.maximum(b_tot + m_prev, jnp.max(g, axis=1, keepdims=True))
            w_c = jnp.exp(b_tot + m_prev - m_new)
            w_s = jnp.exp(g - m_new)
            vw = jnp.concatenate([vt * w_s, ones_blk * w_s], axis=0).astype(BF16)
            ctn_ref[h] = w_c[:, :d] * ctn + _dot(vw, kc)
            m_ref[h] = jnp.broadcast_to(m_new, m_ref.shape[1:])
        return 0

    lax.fori_loop(0, nchunk, chunk, 0)


def _mlstm(qb, kb, vbt, ob, grow, gcol, norm_g):
    bn, s, w = qb.shape
    ts = TS_MLSTM
    nc = ts // MLSTM_L
    tok = pl.BlockSpec((1, ts, w), lambda b, t: (b, t, 0))
    return pl.pallas_call(
        _mlstm_kernel,
        grid=(bn, s // ts),
        in_specs=[
            tok, tok,
            pl.BlockSpec((1, nc, w, MLSTM_L), lambda b, t: (b, t, 0, 0)),
            tok,
            pl.BlockSpec((1, nc, 2 * B_HEADS, MLSTM_L), lambda b, t: (b, t, 0, 0)),
            pl.BlockSpec((1, ts, 2 * B_HEADS), lambda b, t: (b, t, 0)),
            _const_spec((1, w)),
        ],
        out_specs=tok,
        out_shape=jax.ShapeDtypeStruct((bn, s, w), BF16),
        scratch_shapes=[pltpu.VMEM((B_HEADS, AUG_ROWS, B_HEAD_DIM), F32),
                        pltpu.VMEM((B_HEADS, SUBLANES, MLSTM_L), F32)],
        compiler_params=pltpu.CompilerParams(
            dimension_semantics=("arbitrary", "arbitrary"), vmem_limit_bytes=VMEM_LIMIT_BYTES),
        name="mlstm_chunkwise",
    )(qb, kb, vbt, ob, grow, gcol, norm_g)


def _merge_ffn_kernel(x_ref, ya_ref, yb_ref, sga_ref, sgb_ref, wpa_ref, wpb_ref, wout_ref,
                      gffn_ref, wgate_ref, wup_ref, wdown_ref, gfin_ref, o_ref, u_ref, acc_ref):
    merged = (sga_ref[0].astype(F32) * _dot(ya_ref[0], wpa_ref[...])
              + sgb_ref[0].astype(F32) * _dot(yb_ref[0], wpb_ref[...]))
    h = x_ref[0] + _dot(merged.astype(BF16), wout_ref[...])
    var = jnp.mean(h * h, axis=-1, keepdims=True)
    u_ref[...] = (h * lax.rsqrt(var + NORM_EPS) * gffn_ref[...]).astype(BF16)
    acc_ref[...] = h

    def ff(c, _):
        u = u_ref[...]
        g = _dot(u, wgate_ref[c])
        up = _dot(u, wup_ref[c])
        act = (g * jax.nn.sigmoid(g) * up).astype(BF16)
        acc_ref[...] += _dot(act, wdown_ref[c])
        return 0

    lax.fori_loop(0, wgate_ref.shape[0], ff, 0)
    h2 = acc_ref[...]
    var2 = jnp.mean(h2 * h2, axis=-1, keepdims=True)
    o_ref[0] = h2 * lax.rsqrt(var2 + NORM_EPS) * gfin_ref[...]


def _merge_ffn(x, ya, yb, sga, sgb, wpa, wpb, wout, gffn, wgate, wup, wdown, gfin):
    bn, s, d = x.shape
    tm = TM_OUT
    tok = lambda w: pl.BlockSpec((1, tm, w), lambda b, t: (b, t, 0))
    return pl.pallas_call(
        _merge_ffn_kernel,
        grid=(bn, s // tm),
        in_specs=[tok(d), tok(A_WIDTH), tok(B_WIDTH), tok(d), tok(d),
                  _const_spec(wpa.shape), _const_spec(wpb.shape), _const_spec(wout.shape),
                  _const_spec(gffn.shape), _const_spec(wgate.shape), _const_spec(wup.shape),
                  _const_spec(wdown.shape), _const_spec(gfin.shape)],
        out_specs=tok(d),
        out_shape=jax.ShapeDtypeStruct((bn, s, d), F32),
        scratch_shapes=[pltpu.VMEM((tm, d), BF16), pltpu.VMEM((tm, d), F32)],
        compiler_params=pltpu.CompilerParams(
            dimension_semantics=("arbitrary", "arbitrary"), vmem_limit_bytes=VMEM_LIMIT_BYTES),
        name="merge_ffn",
    )(x, ya, yb, sga, sgb, wpa, wpb, wout, gffn, wgate, wup, wdown, gfin)


def kernel(x, norm_mix_g, w_in, conv_w, b_igate, b_fgate, mlstm_norm_g, w_proj_a, w_proj_b,
           w_out, norm_ffn_g, w_gate_up, w_down, norm_final_g):
    bn, s, d = x.shape
    assert d == D_MODEL and s % TS_MLSTM == 0 and s % TM_IN == 0 and s % TM_OUT == 0
    assert norm_mix_g.shape[0] == 1, "single layer"
    w = w_in[0]
    o_qa, o_ka, o_va = 0, A_WIDTH, 2 * A_WIDTH
    o_qkb = 3 * A_WIDTH
    o_vb = o_qkb + 2 * B_WIDTH
    o_ob = o_vb + B_WIDTH
    o_i = o_ob + B_WIDTH
    o_f = o_i + B_HEADS
    o_ga = o_f + B_HEADS
    o_gb = o_ga + D_MODEL

    wqka = w[:, o_qa:o_va].astype(BF16)
    wqkb = w[:, o_qkb:o_vb].astype(BF16)
    wo = w[:, o_ob:o_i].astype(BF16)
    wg = w[:, o_ga:o_gb + D_MODEL].astype(BF16)
    w_fi = jnp.concatenate([w[:, o_f:o_f + B_HEADS], w[:, o_i:o_i + B_HEADS]], axis=1)
    wif = w_fi.astype(BF16)
    wift = w_fi.T.astype(BF16)
    wvt = jnp.concatenate([w[:, o_va:o_qkb], w[:, o_vb:o_ob]], axis=1).T.astype(BF16)
    bias_fi = jnp.concatenate([b_fgate[0], b_igate[0]]).astype(F32)
    bcol = bias_fi[None, :]
    brow = bias_fi[:, None]

    cos_t, sin_t = _rope_tables(s)
    (qa, ka, kmean, vat, vbt, qb, kb, ob, sga, sgb, grow, gcol) = _in_proj(
        x, norm_mix_g, cos_t, sin_t, conv_w[0], bcol, brow, wqka, wqkb, wo, wg, wif, wvt, wift)
    kmean = kmean.reshape(bn, s // MOBA_BLOCK, A_WIDTH)

    ya = _moba(qa, ka, vat, kmean)
    yb = _mlstm(qb, kb, vbt, ob, grow, gcol, mlstm_norm_g)

    nff = D_FF // FF_CHUNK
    wgu = w_gate_up[0].astype(BF16)
    wgate = wgu[:, :D_FF].reshape(d, nff, FF_CHUNK).transpose(1, 0, 2)
    wup = wgu[:, D_FF:].reshape(d, nff, FF_CHUNK).transpose(1, 0, 2)
    wdown = w_down[0].astype(BF16).reshape(nff, FF_CHUNK, d)
    return _merge_ffn(x, ya, yb, sga, sgb,
                      w_proj_a[0].astype(BF16), w_proj_b[0].astype(BF16), w_out[0].astype(BF16),
                      norm_ffn_g, wgate, wup, wdown, norm_final_g[None, :])
```

```python
import functools

import jax
import jax.numpy as jnp
import numpy as np
from jax import lax
from jax.experimental import pallas as pl
from jax.experimental.pallas import tpu as pltpu

F32 = jnp.float32
BF16 = jnp.bfloat16

D_MODEL = 1024
A_HEADS = 8
A_HEAD_DIM = 64
A_WIDTH = A_HEADS * A_HEAD_DIM
MOBA_BLOCK = 256
MOBA_TOPK = 3
ROPE_THETA = 10000.0
B_HEADS = 4
B_HEAD_DIM = 128
B_WIDTH = B_HEADS * B_HEAD_DIM
CONV_WIDTH = 4
D_FF = 2816
NORM_EPS = 1e-6

LANES = 128
SUBLANES = 8
BF16_SUBLANE_PACK = 16
VMEM_LIMIT_BYTES = 56 * 1024 * 1024

TM_IN = 512
MLSTM_L = 256
TS_MLSTM = 1024
TM_OUT = 512
FF_CHUNK = 256
AUG_ROWS = B_HEAD_DIM + BF16_SUBLANE_PACK
MOBA_HEADS_PER_STEP = 4
VA_ROWS = A_HEAD_DIM + BF16_SUBLANE_PACK
LOG2E = 1.4426950408889634


def _dot(a, b):
    return jnp.dot(a, b, preferred_element_type=F32)


def _dot_nt(a, b):
    return lax.dot_general(a, b, (((1,), (1,)), ((), ())), preferred_element_type=F32)


def _split3(x):
    x1 = x.astype(BF16)
    r1 = x - x1.astype(F32)
    x2 = r1.astype(BF16)
    r2 = r1 - x2.astype(F32)
    return x1, x2, r2.astype(BF16)


def _const_spec(shape):
    nd = len(shape)
    return pl.BlockSpec(shape, lambda *_: (0,) * nd, pipeline_mode=pl.Buffered(1))


def _rope_table_kernel(inv_ref, cos_ref, sin_ref):
    s = cos_ref.shape[0]
    pos = lax.broadcasted_iota(jnp.int32, (s, LANES), 0).astype(F32)
    lane = lax.broadcasted_iota(jnp.int32, (s, LANES), 1)
    ang = pos * inv_ref[...]
    first_half = (lane % A_HEAD_DIM) < (A_HEAD_DIM // 2)
    cos_ref[...] = jnp.cos(ang)
    sn = jnp.sin(ang)
    sin_ref[...] = jnp.where(first_half, -sn, sn)


def _rope_tables(s):
    half = A_HEAD_DIM // 2
    inv = ROPE_THETA ** (-jnp.arange(half, dtype=F32) / half)
    inv = jnp.tile(inv, LANES // half)[None, :]
    return pl.pallas_call(
        _rope_table_kernel,
        out_shape=(jax.ShapeDtypeStruct((s, LANES), F32), jax.ShapeDtypeStruct((s, LANES), F32)),
        name="rope_tables",
    )(inv)


def _log_sigmoid(x):
    return jnp.minimum(x, 0.0) - jnp.log(1.0 + jnp.exp(-jnp.abs(x)))


def _inproj_kernel(x_ref, g_ref, cos_ref, sin_ref, convw_ref, bcol_ref, brow_ref,
                   wqka_ref, wqkb_ref, wo_ref, wg_ref, wif_ref, wvt_ref, wift_ref,
                   qa_ref, ka_ref, kmean_ref, vat_ref, vbt_ref, qb_ref, kb_ref, ob_ref,
                   sga_ref, sgb_ref, grow_ref, gcol_ref,
                   u_ref, carry_ref):
    tm = x_ref.shape[1]
    nchunk = tm // MOBA_BLOCK
    t_idx = pl.program_id(1)

    xf = x_ref[0]
    var = jnp.mean(xf * xf, axis=-1, keepdims=True)
    u_ref[...] = (xf * lax.rsqrt(var + NORM_EPS) * g_ref[...]).astype(BF16)
    u = u_ref[...]

    cos = jnp.tile(cos_ref[...], (1, A_WIDTH // LANES))
    sin = jnp.tile(sin_ref[...], (1, A_WIDTH // LANES))
    lane = lax.broadcasted_iota(jnp.int32, (tm, A_WIDTH), 1)
    first_half = (lane % A_HEAD_DIM) < (A_HEAD_DIM // 2)

    def rope(z):
        partner = jnp.where(first_half,
                            pltpu.roll(z, A_WIDTH - A_HEAD_DIM // 2, 1),
                            pltpu.roll(z, A_HEAD_DIM // 2, 1))
        return z * cos + partner * sin

    q = rope(_dot(u, wqka_ref[:, :A_WIDTH]))
    qa_ref[0] = (q * (A_HEAD_DIM ** -0.5 * LOG2E)).astype(BF16)
    k = rope(_dot(u, wqka_ref[:, A_WIDTH:]))
    ka_ref[0] = k.astype(BF16)
    for c in range(nchunk):
        kmean_ref[0, 0, c:c + 1, :] = jnp.mean(k[c * MOBA_BLOCK:(c + 1) * MOBA_BLOCK], axis=0, keepdims=True)

    vat = _dot_nt(wvt_ref[:A_WIDTH, :], u).astype(BF16)
    vbt = _dot_nt(wvt_ref[A_WIDTH:, :], u).astype(BF16)
    ones_row = lax.broadcasted_iota(jnp.int32, (BF16_SUBLANE_PACK, MOBA_BLOCK), 0) == 0
    ones_blk = jnp.where(ones_row, 1.0, 0.0).astype(BF16)
    for c in range(nchunk):
        cs = slice(c * MOBA_BLOCK, (c + 1) * MOBA_BLOCK)
        for h in range(A_HEADS):
            vat_ref[0, c, h * VA_ROWS:h * VA_ROWS + A_HEAD_DIM, :] = vat[h * A_HEAD_DIM:(h + 1) * A_HEAD_DIM, cs]
            vat_ref[0, c, h * VA_ROWS + A_HEAD_DIM:(h + 1) * VA_ROWS, :] = ones_blk
        vbt_ref[0, c] = vbt[:, cs]

    @pl.when(t_idx == 0)
    def _():
        carry_ref[...] = jnp.zeros_like(carry_ref)

    zqk = _dot(u, wqkb_ref[...])
    ext = jnp.concatenate([carry_ref[...], zqk], axis=0)
    carry_ref[...] = zqk[tm - SUBLANES:, :]
    conv = None
    for j in range(CONV_WIDTH):
        off = SUBLANES - (CONV_WIDTH - 1) + j
        term = ext[off:off + tm, :] * convw_ref[j:j + 1, :]
        conv = term if conv is None else conv + term
    act = conv * jax.nn.sigmoid(conv)
    qb_ref[0] = act[:, :B_WIDTH].astype(BF16)
    kb_ref[0] = (act[:, B_WIDTH:] * (B_HEAD_DIM ** -0.5)).astype(BF16)

    ob_ref[0] = _dot(u, wo_ref[...]).astype(BF16)

    sga_ref[0] = jax.nn.sigmoid(_dot(u, wg_ref[:, :D_MODEL])).astype(BF16)
    sgb_ref[0] = jax.nn.sigmoid(_dot(u, wg_ref[:, D_MODEL:])).astype(BF16)

    L = MLSTM_L
    zc = _dot(u, wif_ref[...]) + bcol_ref[...]
    lane8 = lax.broadcasted_iota(jnp.int32, zc.shape, 1)
    pre_c = jnp.where(lane8 < B_HEADS, _log_sigmoid(zc), zc)
    zr = _dot_nt(wift_ref[...], u) + brow_ref[...]
    row8 = lax.broadcasted_iota(jnp.int32, zr.shape, 0)
    pre_r = jnp.where(row8 < B_HEADS, _log_sigmoid(zr), zr)
    ti = lax.broadcasted_iota(jnp.int32, (L, L), 0)
    tj = lax.broadcasted_iota(jnp.int32, (L, L), 1)
    tril = jnp.where(tj <= ti, 1.0, 0.0).astype(BF16)
    triu = jnp.where(ti <= tj, 1.0, 0.0).astype(BF16)
    lane8_c = lax.broadcasted_iota(jnp.int32, (L, 2 * B_HEADS), 1)
    row8_c = lax.broadcasted_iota(jnp.int32, (2 * B_HEADS, L), 0)
    for c in range(tm // L):
        pc = pre_c[c * L:(c + 1) * L, :]
        p1, p2, p3 = _split3(pc)
        cum = _dot(tril, p1) + _dot(tril, p2) + _dot(tril, p3)
        gcol_ref[0, c * L:(c + 1) * L, :] = jnp.where(lane8_c < B_HEADS, cum, pc)
        pr = pre_r[:, c * L:(c + 1) * L]
        r1, r2, r3 = _split3(pr)
        cumr = _dot(r1, triu) + _dot(r2, triu) + _dot(r3, triu)
        grow_ref[0, c] = jnp.where(row8_c < B_HEADS, cumr, pr)


def _in_proj(x, norm_g, cos_t, sin_t, conv_w, bcol, brow, wqka, wqkb, wo, wg, wif, wvt, wift):
    bn, s, d = x.shape
    tm = TM_IN
    nt = s // tm
    nc = tm // MOBA_BLOCK
    nblk = s // MOBA_BLOCK
    tok = lambda w: pl.BlockSpec((1, tm, w), lambda b, t: (b, t, 0))
    blk4 = lambda r: pl.BlockSpec((1, nc, r, MOBA_BLOCK), lambda b, t: (b, t, 0, 0))
    in_specs = [
        tok(d),
        _const_spec((1, d)),
        pl.BlockSpec((tm, LANES), lambda b, t: (t, 0)),
        pl.BlockSpec((tm, LANES), lambda b, t: (t, 0)),
        _const_spec(conv_w.shape), _const_spec(bcol.shape), _const_spec(brow.shape),
        _const_spec(wqka.shape), _const_spec(wqkb.shape), _const_spec(wo.shape),
        _const_spec(wg.shape), _const_spec(wif.shape), _const_spec(wvt.shape), _const_spec(wift.shape),
    ]
    out_shape = (
        jax.ShapeDtypeStruct((bn, s, A_WIDTH), BF16),
        jax.ShapeDtypeStruct((bn, s, A_WIDTH), BF16),
        jax.ShapeDtypeStruct((bn, nt, nc, A_WIDTH), F32),
        jax.ShapeDtypeStruct((bn, nblk, A_HEADS * VA_ROWS, MOBA_BLOCK), BF16),
        jax.ShapeDtypeStruct((bn, nblk, B_WIDTH, MOBA_BLOCK), BF16),
        jax.ShapeDtypeStruct((bn, s, B_WIDTH), BF16),
        jax.ShapeDtypeStruct((bn, s, B_WIDTH), BF16),
        jax.ShapeDtypeStruct((bn, s, B_WIDTH), BF16),
        jax.ShapeDtypeStruct((bn, s, D_MODEL), BF16),
        jax.ShapeDtypeStruct((bn, s, D_MODEL), BF16),
        jax.ShapeDtypeStruct((bn, nblk, 2 * B_HEADS, MOBA_BLOCK), F32),
        jax.ShapeDtypeStruct((bn, s, 2 * B_HEADS), F32),
    )
    out_specs = (
        tok(A_WIDTH), tok(A_WIDTH),
        pl.BlockSpec((1, 1, nc, A_WIDTH), lambda b, t: (b, t, 0, 0)),
        blk4(A_HEADS * VA_ROWS), blk4(B_WIDTH),
        tok(B_WIDTH), tok(B_WIDTH), tok(B_WIDTH),
        tok(D_MODEL), tok(D_MODEL),
        blk4(2 * B_HEADS),
        tok(2 * B_HEADS),
    )
    return pl.pallas_call(
        _inproj_kernel,
        grid=(bn, nt),
        in_specs=in_specs,
        out_specs=out_specs,
        out_shape=out_shape,
        scratch_shapes=[pltpu.VMEM((tm, d), BF16), pltpu.VMEM((SUBLANES, 2 * B_WIDTH), F32)],
        compiler_params=pltpu.CompilerParams(
            dimension_semantics=("arbitrary", "arbitrary"), vmem_limit_bytes=VMEM_LIMIT_BYTES),
        name="in_proj",
    )(x, norm_g, cos_t, sin_t, conv_w, bcol, brow, wqka, wqkb, wo, wg, wif, wvt, wift)


def _moba_kernel(q_ref, k_ref, vt_ref, kmean_ref, o_ref, sel_ref, acc_ref, s_ref):
    blk = MOBA_BLOCK
    nblk = kmean_ref.shape[1]
    hg = q_ref.shape[2] // A_HEAD_DIM
    qi = pl.program_id(2)

    lane = lax.broadcasted_iota(jnp.int32, (blk, LANES), 1)
    lo_half = lane < A_HEAD_DIM

    def pair_lanes(h):
        return slice((h // 2) * LANES, (h // 2 + 1) * LANES)

    def head_rows(h):
        return slice(h * VA_ROWS, (h + 1) * VA_ROWS)

    qh = []
    for h in range(hg):
        qp = q_ref[0, :, pair_lanes(h)]
        qh.append(jnp.where(lo_half if h % 2 == 0 else jnp.logical_not(lo_half), qp, jnp.zeros_like(qp)))

    bidx = lax.broadcasted_iota(jnp.int32, (nblk, blk), 0)
    past = bidx < qi
    for h in range(hg):
        km = kmean_ref[0, :, pair_lanes(h)]
        km_hi = km.astype(BF16)
        km_lo = (km - km_hi.astype(F32)).astype(BF16)
        gate = _dot_nt(km_hi, qh[h]) + _dot_nt(km_lo, qh[h])
        gate = jnp.where(past, gate, -jnp.inf)
        rank = jnp.zeros((nblk, blk), F32)
        for jp in range(nblk):
            gj = gate[jp:jp + 1, :]
            beats = jnp.where(gj > gate, 1.0, jnp.where((gj == gate) & (bidx > jp), 1.0, 0.0))
            rank = rank + beats
        sel_ref[h] = jnp.where(past & (rank < MOBA_TOPK), 1.0, 0.0)

    key_i = lax.broadcasted_iota(jnp.int32, (blk, blk), 0)
    qry_i = lax.broadcasted_iota(jnp.int32, (blk, blk), 1)
    causal = key_i <= qry_i

    def group_max(st):
        return jnp.max(st.reshape(blk // SUBLANES, SUBLANES, blk), axis=0)

    d0 = pl.multiple_of(qi * blk, blk)
    mx0 = []
    for h in range(hg):
        kd = k_ref[0, pl.ds(d0, blk), pair_lanes(h)]
        st = jnp.where(causal, _dot_nt(kd, qh[h]), -jnp.inf)
        s_ref[h, qi] = st
        mx0.append(group_max(st))

    def score_body(j, mx):
        r0 = pl.multiple_of(j * blk, blk)
        out = []
        for h in range(hg):
            kj = k_ref[0, pl.ds(r0, blk), pair_lanes(h)]
            st = jnp.where(sel_ref[h, pl.ds(j, 1), :] > 0.5, _dot_nt(kj, qh[h]), -jnp.inf)
            s_ref[h, j] = st
            out.append(jnp.maximum(mx[h], group_max(st)))
        return tuple(out)

    mx = lax.fori_loop(0, qi, score_body, tuple(mx0))
    m = [jnp.max(mx[h], axis=0, keepdims=True) for h in range(hg)]

    for h in range(hg):
        acc_ref[h] = jnp.zeros(acc_ref.shape[1:], F32)

    def value_body(j, _):
        for h in range(hg):
            p = jnp.exp2(s_ref[h, j] - m[h]).astype(BF16)
            acc_ref[h] += _dot(vt_ref[0, j, head_rows(h), :], p)
        return 0

    lax.fori_loop(0, qi + 1, value_body, 0)

    outs = []
    for h in range(hg):
        acc = acc_ref[h]
        outs.append(acc[:A_HEAD_DIM] / acc[A_HEAD_DIM:A_HEAD_DIM + 1])
    o_ref[0] = jnp.concatenate(outs, axis=0).T.astype(BF16)


def _moba(qa, ka, vat, kmean):
    bn, s, _ = qa.shape
    blk = MOBA_BLOCK
    nblk = s // blk
    hg = MOBA_HEADS_PER_STEP
    ngroup = A_HEADS // hg
    gw = hg * A_HEAD_DIM
    return pl.pallas_call(
        _moba_kernel,
        grid=(bn, ngroup, nblk),
        in_specs=[
            pl.BlockSpec((1, blk, gw), lambda b, g, i: (b, i, g)),
            pl.BlockSpec((1, s, gw), lambda b, g, i: (b, 0, g)),
            pl.BlockSpec((1, nblk, hg * VA_ROWS, blk), lambda b, g, i: (b, 0, g, 0)),
            pl.BlockSpec((1, nblk, gw), lambda b, g, i: (b, 0, g)),
        ],
        out_specs=pl.BlockSpec((1, blk, gw), lambda b, g, i: (b, i, g)),
        out_shape=jax.ShapeDtypeStruct((bn, s, A_WIDTH), BF16),
        scratch_shapes=[pltpu.VMEM((hg, nblk, blk), F32),
                        pltpu.VMEM((hg, VA_ROWS, blk), F32),
                        pltpu.VMEM((hg, nblk, blk, blk), F32)],
        compiler_params=pltpu.CompilerParams(
            dimension_semantics=("arbitrary", "arbitrary", "arbitrary"), vmem_limit_bytes=VMEM_LIMIT_BYTES),
        name="moba_attention",
    )(qa, ka, vat, kmean)


def _mlstm_kernel(q_ref, k_ref, vt_ref, o_ref, grow_ref, gcol_ref, ng_ref, y_ref, ctn_ref, m_ref):
    L = MLSTM_L
    d = B_HEAD_DIM
    nchunk = q_ref.shape[1] // L

    @pl.when(pl.program_id(1) == 0)
    def _():
        ctn_ref[...] = jnp.zeros_like(ctn_ref)
        m_ref[...] = jnp.zeros_like(m_ref)

    s_i = lax.broadcasted_iota(jnp.int32, (L, L), 0)
    t_i = lax.broadcasted_iota(jnp.int32, (L, L), 1)
    causal = s_i <= t_i
    aug_row = lax.broadcasted_iota(jnp.int32, (BF16_SUBLANE_PACK, L), 0)
    ones_blk = jnp.where(aug_row == 0, 1.0, 0.0)

    def chunk(c, _):
        r0 = pl.multiple_of(c * L, L)
        grow = grow_ref[0, c]
        gcol = gcol_ref[0, pl.ds(r0, L), :]
        for h in range(B_HEADS):
            hs = slice(h * d, (h + 1) * d)
            qc = q_ref[0, pl.ds(r0, L), hs]
            kc = k_ref[0, pl.ds(r0, L), hs]
            vt = vt_ref[0, c, hs, :].astype(F32)
            b_row = grow[h:h + 1, :]
            i_row = grow[B_HEADS + h:B_HEADS + h + 1, :]
            r_col = gcol[:, B_HEADS + h:B_HEADS + h + 1] - gcol[:, h:h + 1]
            m_prev = m_ref[h, 0:1, :]

            dm = jnp.where(causal, r_col + b_row, -jnp.inf)
            m_inter = b_row + m_prev
            m_t = jnp.maximum(m_inter, jnp.max(dm, axis=0, keepdims=True))
            pw = (_dot_nt(kc, qc) * jnp.exp(dm - m_t)).astype(BF16)
            w_inter = jnp.exp(m_inter - m_t)
            vt_aug = jnp.concatenate([vt, ones_blk], axis=0).astype(BF16)
            ctn = ctn_ref[h]
            tot = _dot(vt_aug, pw) + w_inter * _dot_nt(ctn.astype(BF16), qc)
            den = tot[d:d + 1, :]
            ht = tot[:d, :] / jnp.maximum(jnp.abs(den), jnp.exp(-m_t))
            ms = jnp.mean(ht * ht, axis=0, keepdims=True)
            hn = (ht * lax.rsqrt(ms + NORM_EPS)).T
            og = jax.nn.sigmoid(o_ref[0, pl.ds(r0, L), hs].astype(F32))
            y_ref[0, pl.ds(r0, L), hs] = (og * (hn * ng_ref[:, hs])).astype(BF16)

            b_tot = b_row[:, L - 1:L]
            g = b_tot - b_row + i_row
            m_new = jnp.maximum(b_tot + m_prev, jnp.max(g, axis=1, keepdims=True))
            w_c = jnp.exp(b_tot + m_prev - m_new)
            w_s = jnp.exp(g - m_new)
            vw = jnp.concatenate([vt * w_s, ones_blk * w_s], axis=0).astype(BF16)
            ctn_ref[h] = w_c[:, :d] * ctn + _dot(vw, kc)
            m_ref[h] = jnp.broadcast_to(m_new, m_ref.shape[1:])
        return 0

    lax.fori_loop(0, nchunk, chunk, 0)


def _mlstm(qb, kb, vbt, ob, grow, gcol, norm_g):
    bn, s, w = qb.shape
    ts = TS_MLSTM
    nc = ts // MLSTM_L
    tok = pl.BlockSpec((1, ts, w), lambda b, t: (b, t, 0))
    return pl.pallas_call(
        _mlstm_kernel,
        grid=(bn, s // ts),
        in_specs=[
            tok, tok,
            pl.BlockSpec((1, nc, w, MLSTM_L), lambda b, t: (b, t, 0, 0)),
            tok,
            pl.BlockSpec((1, nc, 2 * B_HEADS, MLSTM_L), lambda b, t: (b, t, 0, 0)),
            pl.BlockSpec((1, ts, 2 * B_HEADS), lambda b, t: (b, t, 0)),
            _const_spec((1, w)),
        ],
        out_specs=tok,
        out_shape=jax.ShapeDtypeStruct((bn, s, w), BF16),
        scratch_shapes=[pltpu.VMEM((B_HEADS, AUG_ROWS, B_HEAD_DIM), F32),
                        pltpu.VMEM((B_HEADS, SUBLANES, MLSTM_L), F32)],
        compiler_params=pltpu.CompilerParams(
            dimension_semantics=("arbitrary", "arbitrary"), vmem_limit_bytes=VMEM_LIMIT_BYTES),
        name="mlstm_chunkwise",
    )(qb, kb, vbt, ob, grow, gcol, norm_g)


def _merge_ffn_kernel(x_ref, ya_ref, yb_ref, sga_ref, sgb_ref, wpa_ref, wpb_ref, wout_ref,
                      gffn_ref, wgate_ref, wup_ref, wdown_ref, gfin_ref, o_ref, u_ref, acc_ref):
    merged = (sga_ref[0].astype(F32) * _dot(ya_ref[0], wpa_ref[...])
              + sgb_ref[0].astype(F32) * _dot(yb_ref[0], wpb_ref[...]))
    h = x_ref[0] + _dot(merged.astype(BF16), wout_ref[...])
    var = jnp.mean(h * h, axis=-1, keepdims=True)
    u_ref[...] = (h * lax.rsqrt(var + NORM_EPS) * gffn_ref[...]).astype(BF16)
    acc_ref[...] = h

    def ff(c, _):
        u = u_ref[...]
        g = _dot(u, wgate_ref[c])
        up = _dot(u, wup_ref[c])
        act = (g * jax.nn.sigmoid(g) * up).astype(BF16)
        acc_ref[...] += _dot(act, wdown_ref[c])
        return 0

    lax.fori_loop(0, wgate_ref.shape[0], ff, 0)
    h2 = acc_ref[...]
    var2 = jnp.mean(h2 * h2, axis=-1, keepdims=True)
    o_ref[0] = h2 * lax.rsqrt(var2 + NORM_EPS) * gfin_ref[...]


def _merge_ffn(x, ya, yb, sga, sgb, wpa, wpb, wout, gffn, wgate, wup, wdown, gfin):
    bn, s, d = x.shape
    tm = TM_OUT
    tok = lambda w: pl.BlockSpec((1, tm, w), lambda b, t: (b, t, 0))
    return pl.pallas_call(
        _merge_ffn_kernel,
        grid=(bn, s // tm),
        in_specs=[tok(d), tok(A_WIDTH), tok(B_WIDTH), tok(d), tok(d),
                  _const_spec(wpa.shape), _const_spec(wpb.shape), _const_spec(wout.shape),
                  _const_spec(gffn.shape), _const_spec(wgate.shape), _const_spec(wup.shape),
                  _const_spec(wdown.shape), _const_spec(gfin.shape)],
        out_specs=tok(d),
        out_shape=jax.ShapeDtypeStruct((bn, s, d), F32),
        scratch_shapes=[pltpu.VMEM((tm, d), BF16), pltpu.VMEM((tm, d), F32)],
        compiler_params=pltpu.CompilerParams(
            dimension_semantics=("arbitrary", "arbitrary"), vmem_limit_bytes=VMEM_LIMIT_BYTES),
        name="merge_ffn",
    )(x, ya, yb, sga, sgb, wpa, wpb, wout, gffn, wgate, wup, wdown, gfin)


def kernel(x, norm_mix_g, w_in, conv_w, b_igate, b_fgate, mlstm_norm_g, w_proj_a, w_proj_b,
           w_out, norm_ffn_g, w_gate_up, w_down, norm_final_g):
    bn, s, d = x.shape
    assert d == D_MODEL and s % TS_MLSTM == 0 and s % TM_IN == 0 and s % TM_OUT == 0
    assert norm_mix_g.shape[0] == 1, "single layer"
    w = w_in[0]
    o_qa, o_ka, o_va = 0, A_WIDTH, 2 * A_WIDTH
    o_qkb = 3 * A_WIDTH
    o_vb = o_qkb + 2 * B_WIDTH
    o_ob = o_vb + B_WIDTH
    o_i = o_ob + B_WIDTH
    o_f = o_i + B_HEADS
    o_ga = o_f + B_HEADS
    o_gb = o_ga + D_MODEL

    wqka = w[:, o_qa:o_va].astype(BF16)
    wqkb = w[:, o_qkb:o_vb].astype(BF16)
    wo = w[:, o_ob:o_i].astype(BF16)
    wg = w[:, o_ga:o_gb + D_MODEL].astype(BF16)
    w_fi = jnp.concatenate([w[:, o_f:o_f + B_HEADS], w[:, o_i:o_i + B_HEADS]], axis=1)
    wif = w_fi.astype(BF16)
    wift = w_fi.T.astype(BF16)
    wvt = jnp.concatenate([w[:, o_va:o_qkb], w[:, o_vb:o_ob]], axis=1).T.astype(BF16)
    bias_fi = jnp.concatenate([b_fgate[0], b_igate[0]]).astype(F32)
    bcol = bias_fi[None, :]
    brow = bias_fi[:, None]

    cos_t, sin_t = _rope_tables(s)
    (qa, ka, kmean, vat, vbt, qb, kb, ob, sga, sgb, grow, gcol) = _in_proj(
        x, norm_mix_g, cos_t, sin_t, conv_w[0], bcol, brow, wqka, wqkb, wo, wg, wif, wvt, wift)
    kmean = kmean.reshape(bn, s // MOBA_BLOCK, A_WIDTH)

    ya = _moba(qa, ka, vat, kmean)
    yb = _mlstm(qb, kb, vbt, ob, grow, gcol, mlstm_norm_g)

    nff = D_FF // FF_CHUNK
    wgu = w_gate_up[0].astype(BF16)
    wgate = wgu[:, :D_FF].reshape(d, nff, FF_CHUNK).transpose(1, 0, 2)
    wup = wgu[:, D_FF:].reshape(d, nff, FF_CHUNK).transpose(1, 0, 2)
    wdown = w_down[0].astype(BF16).reshape(nff, FF_CHUNK, d)
    return _merge_ffn(x, ya, yb, sga, sgb,
                      w_proj_a[0].astype(BF16), w_proj_b[0].astype(BF16), w_out[0].astype(BF16),
                      norm_ffn_g, wgate, wup, wdown, norm_final_g[None, :])
```

```python
import functools

import jax
import jax.numpy as jnp
import numpy as np
from jax import lax
from jax.experimental import pallas as pl
from jax.experimental.pallas import tpu as pltpu

F32 = jnp.float32
BF16 = jnp.bfloat16

D_MODEL = 1024
A_HEADS = 8
A_HEAD_DIM = 64
A_WIDTH = A_HEADS * A_HEAD_DIM
MOBA_BLOCK = 256
MOBA_TOPK = 3
ROPE_THETA = 10000.0
B_HEADS = 4
B_HEAD_DIM = 128
B_WIDTH = B_HEADS * B_HEAD_DIM
CONV_WIDTH = 4
D_FF = 2816
NORM_EPS = 1e-6

LANES = 128
SUBLANES = 8
BF16_SUBLANE_PACK = 16
VMEM_LIMIT_BYTES = 56 * 1024 * 1024

TM_IN = 512
MLSTM_L = 256
TS_MLSTM = 1024
TM_OUT = 512
FF_CHUNK = 256
AUG_ROWS = B_HEAD_DIM + BF16_SUBLANE_PACK
MOBA_HEADS_PER_STEP = 4
VA_ROWS = A_HEAD_DIM + BF16_SUBLANE_PACK
LOG2E = 1.4426950408889634
Q_SCALE = A_HEAD_DIM ** -0.5 * LOG2E
ROPE_TABS = 6


def _dot(a, b):
    return jnp.dot(a, b, preferred_element_type=F32)


def _dot_nt(a, b):
    return lax.dot_general(a, b, (((1,), (1,)), ((), ())), preferred_element_type=F32)


def _split3(x):
    x1 = x.astype(BF16)
    r1 = x - x1.astype(F32)
    x2 = r1.astype(BF16)
    r2 = r1 - x2.astype(F32)
    return x1, x2, r2.astype(BF16)


def _const_spec(shape):
    nd = len(shape)
    return pl.BlockSpec(shape, lambda *_: (0,) * nd, pipeline_mode=pl.Buffered(1))


def _rope_table_kernel(inv_ref, tab_ref):
    s = tab_ref.shape[0]
    pos = lax.broadcasted_iota(jnp.int32, (s, LANES), 0).astype(F32)
    lane = lax.broadcasted_iota(jnp.int32, (s, LANES), 1)
    ang = pos * inv_ref[...]
    first_half = (lane % A_HEAD_DIM) < (A_HEAD_DIM // 2)
    cs = jnp.cos(ang)
    sn = jnp.sin(ang)
    parts = [cs, jnp.where(first_half, -sn, 0.0), jnp.where(first_half, 0.0, sn)]
    for i, p in enumerate(parts):
        tab_ref[:, i * LANES:(i + 1) * LANES] = p
        tab_ref[:, (i + 3) * LANES:(i + 4) * LANES] = p * Q_SCALE


def _rope_tables(s):
    half = A_HEAD_DIM // 2
    inv = ROPE_THETA ** (-jnp.arange(half, dtype=F32) / half)
    inv = jnp.tile(inv, LANES // half)[None, :]
    return pl.pallas_call(
        _rope_table_kernel,
        out_shape=jax.ShapeDtypeStruct((s, ROPE_TABS * LANES), F32),
        name="rope_tables",
    )(inv)


def _log_sigmoid(x):
    return jnp.minimum(x, 0.0) - jnp.log(1.0 + jnp.exp(-jnp.abs(x)))


def _inproj_kernel(x_ref, g_ref, rope_ref, convw_ref, bcol_ref, brow_ref,
                   wqka_ref, wqkb_ref, wo_ref, wg_ref, wif_ref, wvt_ref, wift_ref,
                   qa_ref, ka_ref, kmean_ref, vat_ref, vbt_ref, qb_ref, kb_ref, ob_ref,
                   sga_ref, sgb_ref, grow_ref, gcol_ref,
                   u_ref, zs_ref):
    tm = x_ref.shape[1]
    nchunk = tm // MOBA_BLOCK
    t_idx = pl.program_id(1)

    xf = x_ref[0]
    var = jnp.mean(xf * xf, axis=-1, keepdims=True)
    u_ref[...] = (xf * lax.rsqrt(var + NORM_EPS) * g_ref[...]).astype(BF16)
    u = u_ref[...]

    half = A_HEAD_DIM // 2

    def rope_tiles(z, tab0):
        c, s_lo, s_hi = [rope_ref[:, (tab0 + i) * LANES:(tab0 + i + 1) * LANES] for i in range(3)]
        outs = []
        for lt in range(A_WIDTH // LANES):
            zt = z[:, lt * LANES:(lt + 1) * LANES]
            outs.append(zt * c + pltpu.roll(zt, LANES - half, 1) * s_lo + pltpu.roll(zt, half, 1) * s_hi)
        return outs

    for lt, qt in enumerate(rope_tiles(_dot(u, wqka_ref[:, :A_WIDTH]), 3)):
        qa_ref[0, :, lt * LANES:(lt + 1) * LANES] = qt.astype(BF16)
    for lt, kt in enumerate(rope_tiles(_dot(u, wqka_ref[:, A_WIDTH:]), 0)):
        ls = slice(lt * LANES, (lt + 1) * LANES)
        ka_ref[0, :, ls] = kt.astype(BF16)
        for c in range(nchunk):
            kmean_ref[0, 0, c:c + 1, ls] = jnp.mean(kt[c * MOBA_BLOCK:(c + 1) * MOBA_BLOCK], axis=0, keepdims=True)

    vat = _dot_nt(wvt_ref[:A_WIDTH, :], u).astype(BF16)
    vbt = _dot_nt(wvt_ref[A_WIDTH:, :], u).astype(BF16)
    ones_row = lax.broadcasted_iota(jnp.int32, (BF16_SUBLANE_PACK, MOBA_BLOCK), 0) == 0
    ones_blk = jnp.where(ones_row, 1.0, 0.0).astype(BF16)
    for c in range(nchunk):
        cs = slice(c * MOBA_BLOCK, (c + 1) * MOBA_BLOCK)
        for h in range(A_HEADS):
            vat_ref[0, c, h * VA_ROWS:h * VA_ROWS + A_HEAD_DIM, :] = vat[h * A_HEAD_DIM:(h + 1) * A_HEAD_DIM, cs]
            vat_ref[0, c, h * VA_ROWS + A_HEAD_DIM:(h + 1) * VA_ROWS, :] = ones_blk
        vbt_ref[0, c] = vbt[:, cs]

    @pl.when(t_idx == 0)
    def _():
        zs_ref[:SUBLANES, :] = jnp.zeros((SUBLANES, zs_ref.shape[1]), F32)

    zs_ref[SUBLANES:, :] = _dot(u, wqkb_ref[...])
    conv = None
    for j in range(CONV_WIDTH):
        off = SUBLANES - (CONV_WIDTH - 1) + j
        term = zs_ref[off:off + tm, :] * convw_ref[j:j + 1, :]
        conv = term if conv is None else conv + term
    zs_ref[:SUBLANES, :] = zs_ref[tm:, :]
    act = conv * jax.nn.sigmoid(conv)
    qb_ref[0] = act[:, :B_WIDTH].astype(BF16)
    kb_ref[0] = (act[:, B_WIDTH:] * (B_HEAD_DIM ** -0.5)).astype(BF16)

    ob_ref[0] = _dot(u, wo_ref[...]).astype(BF16)

    sga_ref[0] = jax.nn.sigmoid(_dot(u, wg_ref[:, :D_MODEL])).astype(BF16)
    sgb_ref[0] = jax.nn.sigmoid(_dot(u, wg_ref[:, D_MODEL:])).astype(BF16)

    L = MLSTM_L
    zc = _dot(u, wif_ref[...]) + bcol_ref[...]
    lane8 = lax.broadcasted_iota(jnp.int32, zc.shape, 1)
    pre_c = jnp.where(lane8 < B_HEADS, _log_sigmoid(zc), zc)
    zr = _dot_nt(wift_ref[...], u) + brow_ref[...]
    row8 = lax.broadcasted_iota(jnp.int32, zr.shape, 0)
    pre_r = jnp.where(row8 < B_HEADS, _log_sigmoid(zr), zr)
    ti = lax.broadcasted_iota(jnp.int32, (L, L), 0)
    tj = lax.broadcasted_iota(jnp.int32, (L, L), 1)
    tril = jnp.where(tj <= ti, 1.0, 0.0).astype(BF16)
    triu = jnp.where(ti <= tj, 1.0, 0.0).astype(BF16)
    lane8_c = lax.broadcasted_iota(jnp.int32, (L, 2 * B_HEADS), 1)
    row8_c = lax.broadcasted_iota(jnp.int32, (2 * B_HEADS, L), 0)
    for c in range(tm // L):
        pc = pre_c[c * L:(c + 1) * L, :]
        p1, p2, p3 = _split3(pc)
        cum = _dot(tril, p1) + _dot(tril, p2) + _dot(tril, p3)
        gcol_ref[0, c * L:(c + 1) * L, :] = jnp.where(lane8_c < B_HEADS, cum, pc)
        pr = pre_r[:, c * L:(c + 1) * L]
        r1, r2, r3 = _split3(pr)
        cumr = _dot(r1, triu) + _dot(r2, triu) + _dot(r3, triu)
        grow_ref[0, c] = jnp.where(row8_c < B_HEADS, cumr, pr)


def _in_proj(x, norm_g, rope_t, conv_w, bcol, brow, wqka, wqkb, wo, wg, wif, wvt, wift):
    bn, s, d = x.shape
    tm = TM_IN
    nt = s // tm
    nc = tm // MOBA_BLOCK
    nblk = s // MOBA_BLOCK
    tok = lambda w: pl.BlockSpec((1, tm, w), lambda b, t: (b, t, 0))
    blk4 = lambda r: pl.BlockSpec((1, nc, r, MOBA_BLOCK), lambda b, t: (b, t, 0, 0))
    in_specs = [
        tok(d),
        _const_spec((1, d)),
        pl.BlockSpec((tm, ROPE_TABS * LANES), lambda b, t: (t, 0)),
        _const_spec(conv_w.shape), _const_spec(bcol.shape), _const_spec(brow.shape),
        _const_spec(wqka.shape), _const_spec(wqkb.shape), _const_spec(wo.shape),
        _const_spec(wg.shape), _const_spec(wif.shape), _const_spec(wvt.shape), _const_spec(wift.shape),
    ]
    out_shape = (
        jax.ShapeDtypeStruct((bn, s, A_WIDTH), BF16),
        jax.ShapeDtypeStruct((bn, s, A_WIDTH), BF16),
        jax.ShapeDtypeStruct((bn, nt, nc, A_WIDTH), F32),
        jax.ShapeDtypeStruct((bn, nblk, A_HEADS * VA_ROWS, MOBA_BLOCK), BF16),
        jax.ShapeDtypeStruct((bn, nblk, B_WIDTH, MOBA_BLOCK), BF16),
        jax.ShapeDtypeStruct((bn, s, B_WIDTH), BF16),
        jax.ShapeDtypeStruct((bn, s, B_WIDTH), BF16),
        jax.ShapeDtypeStruct((bn, s, B_WIDTH), BF16),
        jax.ShapeDtypeStruct((bn, s, D_MODEL), BF16),
        jax.ShapeDtypeStruct((bn, s, D_MODEL), BF16),
        jax.ShapeDtypeStruct((bn, nblk, 2 * B_HEADS, MOBA_BLOCK), F32),
        jax.ShapeDtypeStruct((bn, s, 2 * B_HEADS), F32),
    )
    out_specs = (
        tok(A_WIDTH), tok(A_WIDTH),
        pl.BlockSpec((1, 1, nc, A_WIDTH), lambda b, t: (b, t, 0, 0)),
        blk4(A_HEADS * VA_ROWS), blk4(B_WIDTH),
        tok(B_WIDTH), tok(B_WIDTH), tok(B_WIDTH),
        tok(D_MODEL), tok(D_MODEL),
        blk4(2 * B_HEADS),
        tok(2 * B_HEADS),
    )
    return pl.pallas_call(
        _inproj_kernel,
        grid=(bn, nt),
        in_specs=in_specs,
        out_specs=out_specs,
        out_shape=out_shape,
        scratch_shapes=[pltpu.VMEM((tm, d), BF16), pltpu.VMEM((tm + SUBLANES, 2 * B_WIDTH), F32)],
        compiler_params=pltpu.CompilerParams(
            dimension_semantics=("arbitrary", "arbitrary"), vmem_limit_bytes=VMEM_LIMIT_BYTES),
        name="in_proj",
    )(x, norm_g, rope_t, conv_w, bcol, brow, wqka, wqkb, wo, wg, wif, wvt, wift)


def _moba_kernel(q_ref, k_ref, vt_ref, kmean_ref, o_ref, sel_ref, acc_ref, mx_ref, s_ref):
    blk = MOBA_BLOCK
    nblk = kmean_ref.shape[1]
    hg = q_ref.shape[2] // A_HEAD_DIM
    i = pl.program_id(2)

    def pair_lanes(h):
        return slice((h // 2) * LANES, (h // 2 + 1) * LANES)

    def head_rows(h):
        return slice(h * VA_ROWS, (h + 1) * VA_ROWS)

    def group_max(st):
        return jnp.max(st.reshape(blk // SUBLANES, SUBLANES, blk), axis=0)

    @pl.when(i == 0)
    def _():
        mx_ref[...] = jnp.zeros_like(mx_ref)
        o_ref[...] = jnp.zeros_like(o_ref)

    for h in range(hg):
        acc_ref[h] = jnp.zeros(acc_ref.shape[1:], F32)

    def value_fn(prev):
        m_prev = [jnp.max(mx_ref[prev, h], axis=0, keepdims=True) for h in range(hg)]

        def value_tile(h, t):
            p = jnp.exp2(s_ref[prev, h, t] - m_prev[h]).astype(BF16)
            acc_ref[h] += _dot(vt_ref[0, t, head_rows(h), :], p)

        return value_tile

    def scores_and_values(cur):
        value_tile = value_fn(1 - cur)
        lane = lax.broadcasted_iota(jnp.int32, (blk, LANES), 1)
        lo_half = lane < A_HEAD_DIM
        qh = []
        for h in range(hg):
            qp = q_ref[0, :, pair_lanes(h)]
            qh.append(jnp.where(lo_half if h % 2 == 0 else jnp.logical_not(lo_half), qp, jnp.zeros_like(qp)))

        bidx = lax.broadcasted_iota(jnp.int32, (nblk, blk), 0)
        past = bidx < i
        for h in range(hg):
            km = kmean_ref[0, :, pair_lanes(h)]
            km_hi = km.astype(BF16)
            km_lo = (km - km_hi.astype(F32)).astype(BF16)
            gate = _dot_nt(km_hi, qh[h]) + _dot_nt(km_lo, qh[h])
            gate = jnp.where(past, gate, -jnp.inf)
            rank = jnp.zeros((nblk, blk), F32)
            for jp in range(nblk):
                gj = gate[jp:jp + 1, :]
                beats = jnp.where(gj > gate, 1.0, jnp.where((gj == gate) & (bidx > jp), 1.0, 0.0))
                rank = rank + beats
            sel_ref[h] = jnp.where(past & (rank < MOBA_TOPK), 1.0, 0.0)

        key_i = lax.broadcasted_iota(jnp.int32, (blk, blk), 0)
        qry_i = lax.broadcasted_iota(jnp.int32, (blk, blk), 1)
        causal = key_i <= qry_i
        d0 = pl.multiple_of(i * blk, blk)
        mx0 = []
        for h in range(hg):
            kd = k_ref[0, pl.ds(d0, blk), pair_lanes(h)]
            st = jnp.where(causal, _dot_nt(kd, qh[h]), -jnp.inf)
            s_ref[cur, h, i] = st
            mx0.append(group_max(st))

        def body(t, mx):
            r0 = pl.multiple_of(t * blk, blk)
            out = []
            for h in range(hg):
                kt = k_ref[0, pl.ds(r0, blk), pair_lanes(h)]
                st = jnp.where(sel_ref[h, pl.ds(t, 1), :] > 0.5, _dot_nt(kt, qh[h]), -jnp.inf)
                s_ref[cur, h, t] = st
                out.append(jnp.maximum(mx[h], group_max(st)))
                value_tile(h, t)
            return tuple(out)

        mx = lax.fori_loop(0, i, body, tuple(mx0))
        for h in range(hg):
            mx_ref[cur, h] = mx[h]

    for slot in range(2):
        pl.when((i < nblk) & (lax.rem(i, 2) == slot))(functools.partial(scores_and_values, slot))

    @pl.when(i == nblk)
    def _():
        value_tile = value_fn((nblk - 1) % 2)

        def body(t, _):
            for h in range(hg):
                value_tile(h, t)
            return 0

        lax.fori_loop(0, nblk, body, 0)

    @pl.when(i > 0)
    def _():
        outs = []
        for h in range(hg):
            acc = acc_ref[h]
            outs.append(acc[:A_HEAD_DIM] / acc[A_HEAD_DIM:A_HEAD_DIM + 1])
        o_ref[0] = jnp.concatenate(outs, axis=0).T.astype(BF16)


def _moba(qa, ka, vat, kmean):
    bn, s, _ = qa.shape
    blk = MOBA_BLOCK
    nblk = s // blk
    hg = MOBA_HEADS_PER_STEP
    ngroup = A_HEADS // hg
    gw = hg * A_HEAD_DIM
    return pl.pallas_call(
        _moba_kernel,
        grid=(bn, ngroup, nblk + 1),
        in_specs=[
            pl.BlockSpec((1, blk, gw), lambda b, g, i: (b, jnp.minimum(i, nblk - 1), g)),
            pl.BlockSpec((1, s, gw), lambda b, g, i: (b, 0, g)),
            pl.BlockSpec((1, nblk, hg * VA_ROWS, blk), lambda b, g, i: (b, 0, g, 0)),
            pl.BlockSpec((1, nblk, gw), lambda b, g, i: (b, 0, g)),
        ],
        out_specs=pl.BlockSpec((1, blk, gw), lambda b, g, i: (b, jnp.maximum(i - 1, 0), g)),
        out_shape=jax.ShapeDtypeStruct((bn, s, A_WIDTH), BF16),
        scratch_shapes=[pltpu.VMEM((hg, nblk, blk), F32),
                        pltpu.VMEM((hg, VA_ROWS, blk), F32),
                        pltpu.VMEM((2, hg, SUBLANES, blk), F32),
                        pltpu.VMEM((2, hg, nblk, blk, blk), F32)],
        compiler_params=pltpu.CompilerParams(
            dimension_semantics=("arbitrary", "arbitrary", "arbitrary"), vmem_limit_bytes=VMEM_LIMIT_BYTES),
        name="moba_attention",
    )(qa, ka, vat, kmean)


def _mlstm_kernel(q_ref, k_ref, vt_ref, o_ref, grow_ref, gcol_ref, ng_ref, y_ref, ctn_ref, m_ref):
    L = MLSTM_L
    d = B_HEAD_DIM
    nchunk = q_ref.shape[1] // L

    @pl.when(pl.program_id(1) == 0)
    def _():
        ctn_ref[...] = jnp.zeros_like(ctn_ref)
        m_ref[...] = jnp.zeros_like(m_ref)

    s_i = lax.broadcasted_iota(jnp.int32, (L, L), 0)
    t_i = lax.broadcasted_iota(jnp.int32, (L, L), 1)
    causal = s_i <= t_i
    aug_row = lax.broadcasted_iota(jnp.int32, (BF16_SUBLANE_PACK, L), 0)
    ones_blk = jnp.where(aug_row == 0, 1.0, 0.0)

    def chunk(c, _):
        r0 = pl.multiple_of(c * L, L)
        grow = grow_ref[0, c]
        gcol = gcol_ref[0, pl.ds(r0, L), :]
        for h in range(B_HEADS):
            hs = slice(h * d, (h + 1) * d)
            qc = q_ref[0, pl.ds(r0, L), hs]
            kc = k_ref[0, pl.ds(r0, L), hs]
            vt = vt_ref[0, c, hs, :].astype(F32)
            b_row = grow[h:h + 1, :]
            i_row = grow[B_HEADS + h:B_HEADS + h + 1, :]
            r_col = gcol[:, B_HEADS + h:B_HEADS + h + 1] - gcol[:, h:h + 1]
            m_prev = m_ref[h, 0:1, :]

            dm = jnp.where(causal, r_col + b_row, -jnp.inf)
            m_inter = b_row + m_prev
            m_t = jnp.maximum(m_inter, jnp.max(dm, axis=0, keepdims=True))
            pw = (_dot_nt(kc, qc) * jnp.exp(dm - m_t)).astype(BF16)
            w_inter = jnp.exp(m_inter - m_t)
            vt_aug = jnp.concatenate([vt, ones_blk], axis=0).astype(BF16)
            ctn = ctn_ref[h]
            tot = _dot(vt_aug, pw) + w_inter * _dot_nt(ctn.astype(BF16), qc)
            den = tot[d:d + 1, :]
            ht = tot[:d, :] / jnp.maximum(jnp.abs(den), jnp.exp(-m_t))
            ms = jnp.mean(ht * ht, axis=0, keepdims=True)
            hn = (ht * lax.rsqrt(ms + NORM_EPS)).T
            og = jax.nn.sigmoid(o_ref[0, pl.ds(r0, L), hs].astype(F32))
            y_ref[0, pl.ds(r0, L), hs] = (og * (hn * ng_ref[:, hs])).astype(BF16)

            b_tot = b_row[:, L - 1:L]
            g = b_tot - b_row + i_row
            m_new = jnp.maximum(b_tot + m_prev, jnp.max(g, axis=1, keepdims=True))
            w_c = jnp.exp(b_tot + m_prev - m_new)
            w_s = jnp.exp(g - m_new)
            vw = jnp.concatenate([vt * w_s, ones_blk * w_s], axis=0).astype(BF16)
            ctn_ref[h] = w_c[:, :d] * ctn + _dot(vw, kc)
            m_ref[h] = jnp.broadcast_to(m_new, m_ref.shape[1:])
        return 0

    lax.fori_loop(0, nchunk, chunk, 0)


def _mlstm(qb, kb, vbt, ob, grow, gcol, norm_g):
    bn, s, w = qb.shape
    ts = TS_MLSTM
    nc = ts // MLSTM_L
    tok = pl.BlockSpec((1, ts, w), lambda b, t: (b, t, 0))
    return pl.pallas_call(
        _mlstm_kernel,
        grid=(bn, s // ts),
        in_specs=[
            tok, tok,
            pl.BlockSpec((1, nc, w, MLSTM_L), lambda b, t: (b, t, 0, 0)),
            tok,
            pl.BlockSpec((1, nc, 2 * B_HEADS, MLSTM_L), lambda b, t: (b, t, 0, 0)),
            pl.BlockSpec((1, ts, 2 * B_HEADS), lambda b, t: (b, t, 0)),
            _const_spec((1, w)),
        ],
        out_specs=tok,
        out_shape=jax.ShapeDtypeStruct((bn, s, w), BF16),
        scratch_shapes=[pltpu.VMEM((B_HEADS, AUG_ROWS, B_HEAD_DIM), F32),
                        pltpu.VMEM((B_HEADS, SUBLANES, MLSTM_L), F32)],
        compiler_params=pltpu.CompilerParams(
            dimension_semantics=("arbitrary", "arbitrary"), vmem_limit_bytes=VMEM_LIMIT_BYTES),
        name="mlstm_chunkwise",
    )(qb, kb, vbt, ob, grow, gcol, norm_g)


def _merge_ffn_kernel(x_ref, ya_ref, yb_ref, sga_ref, sgb_ref, wpa_ref, wpb_ref, wout_ref,
                      gffn_ref, wgate_ref, wup_ref, wdown_ref, gfin_ref, o_ref, u_ref, acc_ref):
    merged = (sga_ref[0].astype(F32) * _dot(ya_ref[0], wpa_ref[...])
              + sgb_ref[0].astype(F32) * _dot(yb_ref[0], wpb_ref[...]))
    h = x_ref[0] + _dot(merged.astype(BF16), wout_ref[...])
    var = jnp.mean(h * h, axis=-1, keepdims=True)
    u_ref[...] = (h * lax.rsqrt(var + NORM_EPS) * gffn_ref[...]).astype(BF16)
    acc_ref[...] = h

    def ff(c, _):
        u = u_ref[...]
        g = _dot(u, wgate_ref[c])
        up = _dot(u, wup_ref[c])
        act = (g * jax.nn.sigmoid(g) * up).astype(BF16)
        acc_ref[...] += _dot(act, wdown_ref[c])
        return 0

    lax.fori_loop(0, wgate_ref.shape[0], ff, 0)
    h2 = acc_ref[...]
    var2 = jnp.mean(h2 * h2, axis=-1, keepdims=True)
    o_ref[0] = h2 * lax.rsqrt(var2 + NORM_EPS) * gfin_ref[...]


def _merge_ffn(x, ya, yb, sga, sgb, wpa, wpb, wout, gffn, wgate, wup, wdown, gfin):
    bn, s, d = x.shape
    tm = TM_OUT
    tok = lambda w: pl.BlockSpec((1, tm, w), lambda b, t: (b, t, 0))
    return pl.pallas_call(
        _merge_ffn_kernel,
        grid=(bn, s // tm),
        in_specs=[tok(d), tok(A_WIDTH), tok(B_WIDTH), tok(d), tok(d),
                  _const_spec(wpa.shape), _const_spec(wpb.shape), _const_spec(wout.shape),
                  _const_spec(gffn.shape), _const_spec(wgate.shape), _const_spec(wup.shape),
                  _const_spec(wdown.shape), _const_spec(gfin.shape)],
        out_specs=tok(d),
        out_shape=jax.ShapeDtypeStruct((bn, s, d), F32),
        scratch_shapes=[pltpu.VMEM((tm, d), BF16), pltpu.VMEM((tm, d), F32)],
        compiler_params=pltpu.CompilerParams(
            dimension_semantics=("arbitrary", "arbitrary"), vmem_limit_bytes=VMEM_LIMIT_BYTES),
        name="merge_ffn",
    )(x, ya, yb, sga, sgb, wpa, wpb, wout, gffn, wgate, wup, wdown, gfin)


def kernel(x, norm_mix_g, w_in, conv_w, b_igate, b_fgate, mlstm_norm_g, w_proj_a, w_proj_b,
           w_out, norm_ffn_g, w_gate_up, w_down, norm_final_g):
    bn, s, d = x.shape
    assert d == D_MODEL and s % TS_MLSTM == 0 and s % TM_IN == 0 and s % TM_OUT == 0
    assert norm_mix_g.shape[0] == 1, "single layer"
    w = w_in[0]
    o_qa, o_ka, o_va = 0, A_WIDTH, 2 * A_WIDTH
    o_qkb = 3 * A_WIDTH
    o_vb = o_qkb + 2 * B_WIDTH
    o_ob = o_vb + B_WIDTH
    o_i = o_ob + B_WIDTH
    o_f = o_i + B_HEADS
    o_ga = o_f + B_HEADS
    o_gb = o_ga + D_MODEL

    wqka = w[:, o_qa:o_va].astype(BF16)
    wqkb = w[:, o_qkb:o_vb].astype(BF16)
    wo = w[:, o_ob:o_i].astype(BF16)
    wg = w[:, o_ga:o_gb + D_MODEL].astype(BF16)
    w_fi = jnp.concatenate([w[:, o_f:o_f + B_HEADS], w[:, o_i:o_i + B_HEADS]], axis=1)
    wif = w_fi.astype(BF16)
    wift = w_fi.T.astype(BF16)
    wvt = jnp.concatenate([w[:, o_va:o_qkb], w[:, o_vb:o_ob]], axis=1).T.astype(BF16)
    bias_fi = jnp.concatenate([b_fgate[0], b_igate[0]]).astype(F32)
    bcol = bias_fi[None, :]
    brow = bias_fi[:, None]

    rope_t = _rope_tables(s)
    (qa, ka, kmean, vat, vbt, qb, kb, ob, sga, sgb, grow, gcol) = _in_proj(
        x, norm_mix_g, rope_t, conv_w[0], bcol, brow, wqka, wqkb, wo, wg, wif, wvt, wift)
    kmean = kmean.reshape(bn, s // MOBA_BLOCK, A_WIDTH)

    ya = _moba(qa, ka, vat, kmean)
    yb = _mlstm(qb, kb, vbt, ob, grow, gcol, mlstm_norm_g)

    nff = D_FF // FF_CHUNK
    wgu = w_gate_up[0].astype(BF16)
    wgate = wgu[:, :D_FF].reshape(d, nff, FF_CHUNK).transpose(1, 0, 2)
    wup = wgu[:, D_FF:].reshape(d, nff, FF_CHUNK).transpose(1, 0, 2)
    wdown = w_down[0].astype(BF16).reshape(nff, FF_CHUNK, d)
    return _merge_ffn(x, ya, yb, sga, sgb,
                      w_proj_a[0].astype(BF16), w_proj_b[0].astype(BF16), w_out[0].astype(BF16),
                      norm_ffn_g, wgate, wup, wdown, norm_final_g[None, :])
```

```python
import functools

import jax
import jax.numpy as jnp
import numpy as np
from jax import lax
from jax.experimental import pallas as pl
from jax.experimental.pallas import tpu as pltpu

F32 = jnp.float32
BF16 = jnp.bfloat16

D_MODEL = 1024
A_HEADS = 8
A_HEAD_DIM = 64
A_WIDTH = A_HEADS * A_HEAD_DIM
MOBA_BLOCK = 256
MOBA_TOPK = 3
ROPE_THETA = 10000.0
B_HEADS = 4
B_HEAD_DIM = 128
B_WIDTH = B_HEADS * B_HEAD_DIM
CONV_WIDTH = 4
D_FF = 2816
NORM_EPS = 1e-6

LANES = 128
SUBLANES = 8
BF16_SUBLANE_PACK = 16
VMEM_LIMIT_BYTES = 56 * 1024 * 1024

TM_IN = 512
MLSTM_L = 256
TS_MLSTM = 1024
TM_OUT = 512
FF_CHUNK = 256
AUG_ROWS = B_HEAD_DIM + BF16_SUBLANE_PACK
MOBA_HEADS_PER_STEP = 4
VA_ROWS = A_HEAD_DIM + BF16_SUBLANE_PACK
LOG2E = 1.4426950408889634
Q_SCALE = A_HEAD_DIM ** -0.5 * LOG2E
ROPE_TABS = 6


def _dot(a, b):
    return jnp.dot(a, b, preferred_element_type=F32)


def _dot_nt(a, b):
    return lax.dot_general(a, b, (((1,), (1,)), ((), ())), preferred_element_type=F32)


def _split3(x):
    x1 = x.astype(BF16)
    r1 = x - x1.astype(F32)
    x2 = r1.astype(BF16)
    r2 = r1 - x2.astype(F32)
    return x1, x2, r2.astype(BF16)


def _const_spec(shape):
    nd = len(shape)
    return pl.BlockSpec(shape, lambda *_: (0,) * nd, pipeline_mode=pl.Buffered(1))


def _rope_table_kernel(inv_ref, tab_ref):
    s = tab_ref.shape[0]
    pos = lax.broadcasted_iota(jnp.int32, (s, LANES), 0).astype(F32)
    lane = lax.broadcasted_iota(jnp.int32, (s, LANES), 1)
    ang = pos * inv_ref[...]
    first_half = (lane % A_HEAD_DIM) < (A_HEAD_DIM // 2)
    cs = jnp.cos(ang)
    sn = jnp.sin(ang)
    parts = [cs, jnp.where(first_half, -sn, 0.0), jnp.where(first_half, 0.0, sn)]
    for i, p in enumerate(parts):
        tab_ref[:, i * LANES:(i + 1) * LANES] = p
        tab_ref[:, (i + 3) * LANES:(i + 4) * LANES] = p * Q_SCALE


def _rope_tables(s):
    half = A_HEAD_DIM // 2
    inv = ROPE_THETA ** (-jnp.arange(half, dtype=F32) / half)
    inv = jnp.tile(inv, LANES // half)[None, :]
    return pl.pallas_call(
        _rope_table_kernel,
        out_shape=jax.ShapeDtypeStruct((s, ROPE_TABS * LANES), F32),
        name="rope_tables",
    )(inv)


def _log_sigmoid(x):
    return jnp.minimum(x, 0.0) - jnp.log(1.0 + jnp.exp(-jnp.abs(x)))


def _inproj_kernel(x_ref, g_ref, rope_ref, convw_ref, bcol_ref, brow_ref,
                   wqka_ref, wqkb_ref, wo_ref, wg_ref, wif_ref, wvt_ref, wift_ref,
                   qa_ref, ka_ref, kmean_ref, vat_ref, vbt_ref, qb_ref, kb_ref, ob_ref,
                   sga_ref, sgb_ref, grow_ref, gcol_ref,
                   u_ref, zs_ref):
    tm = x_ref.shape[1]
    nchunk = tm // MOBA_BLOCK
    t_idx = pl.program_id(1)

    xf = x_ref[0]
    var = jnp.mean(xf * xf, axis=-1, keepdims=True)
    u_ref[...] = (xf * lax.rsqrt(var + NORM_EPS) * g_ref[...]).astype(BF16)
    u = u_ref[...]

    half = A_HEAD_DIM // 2

    def rope_tiles(z, tab0):
        c, s_lo, s_hi = [rope_ref[:, (tab0 + i) * LANES:(tab0 + i + 1) * LANES] for i in range(3)]
        outs = []
        for lt in range(A_WIDTH // LANES):
            zt = z[:, lt * LANES:(lt + 1) * LANES]
            outs.append(zt * c + pltpu.roll(zt, LANES - half, 1) * s_lo + pltpu.roll(zt, half, 1) * s_hi)
        return outs

    for lt, qt in enumerate(rope_tiles(_dot(u, wqka_ref[:, :A_WIDTH]), 3)):
        qa_ref[0, :, lt * LANES:(lt + 1) * LANES] = qt.astype(BF16)
    for lt, kt in enumerate(rope_tiles(_dot(u, wqka_ref[:, A_WIDTH:]), 0)):
        ls = slice(lt * LANES, (lt + 1) * LANES)
        ka_ref[0, :, ls] = kt.astype(BF16)
        for c in range(nchunk):
            kmean_ref[0, 0, c:c + 1, ls] = jnp.mean(kt[c * MOBA_BLOCK:(c + 1) * MOBA_BLOCK], axis=0, keepdims=True)

    vat = _dot_nt(wvt_ref[:A_WIDTH, :], u).astype(BF16)
    vbt = _dot_nt(wvt_ref[A_WIDTH:, :], u).astype(BF16)
    ones_row = lax.broadcasted_iota(jnp.int32, (BF16_SUBLANE_PACK, MOBA_BLOCK), 0) == 0
    ones_blk = jnp.where(ones_row, 1.0, 0.0).astype(BF16)
    for c in range(nchunk):
        cs = slice(c * MOBA_BLOCK, (c + 1) * MOBA_BLOCK)
        for h in range(A_HEADS):
            vat_ref[0, c, h * VA_ROWS:h * VA_ROWS + A_HEAD_DIM, :] = vat[h * A_HEAD_DIM:(h + 1) * A_HEAD_DIM, cs]
            vat_ref[0, c, h * VA_ROWS + A_HEAD_DIM:(h + 1) * VA_ROWS, :] = ones_blk
        vbt_ref[0, c] = vbt[:, cs]

    @pl.when(t_idx == 0)
    def _():
        zs_ref[:SUBLANES, :] = jnp.zeros((SUBLANES, zs_ref.shape[1]), F32)

    zs_ref[SUBLANES:, :] = _dot(u, wqkb_ref[...])
    conv = None
    for j in range(CONV_WIDTH):
        off = SUBLANES - (CONV_WIDTH - 1) + j
        term = zs_ref[off:off + tm, :] * convw_ref[j:j + 1, :]
        conv = term if conv is None else conv + term
    zs_ref[:SUBLANES, :] = zs_ref[tm:, :]
    act = conv * jax.nn.sigmoid(conv)
    qb_ref[0] = act[:, :B_WIDTH].astype(BF16)
    kb_ref[0] = (act[:, B_WIDTH:] * (B_HEAD_DIM ** -0.5)).astype(BF16)

    ob_ref[0] = _dot(u, wo_ref[...]).astype(BF16)

    sga_ref[0] = jax.nn.sigmoid(_dot(u, wg_ref[:, :D_MODEL])).astype(BF16)
    sgb_ref[0] = jax.nn.sigmoid(_dot(u, wg_ref[:, D_MODEL:])).astype(BF16)

    L = MLSTM_L
    zc = _dot(u, wif_ref[...]) + bcol_ref[...]
    lane8 = lax.broadcasted_iota(jnp.int32, zc.shape, 1)
    pre_c = jnp.where(lane8 < B_HEADS, _log_sigmoid(zc), zc)
    zr = _dot_nt(wift_ref[...], u) + brow_ref[...]
    row8 = lax.broadcasted_iota(jnp.int32, zr.shape, 0)
    pre_r = jnp.where(row8 < B_HEADS, _log_sigmoid(zr), zr)
    ti = lax.broadcasted_iota(jnp.int32, (L, L), 0)
    tj = lax.broadcasted_iota(jnp.int32, (L, L), 1)
    tril = jnp.where(tj <= ti, 1.0, 0.0).astype(BF16)
    triu = jnp.where(ti <= tj, 1.0, 0.0).astype(BF16)
    lane8_c = lax.broadcasted_iota(jnp.int32, (L, 2 * B_HEADS), 1)
    row8_c = lax.broadcasted_iota(jnp.int32, (2 * B_HEADS, L), 0)
    for c in range(tm // L):
        pc = pre_c[c * L:(c + 1) * L, :]
        p1, p2, p3 = _split3(pc)
        cum = _dot(tril, p1) + _dot(tril, p2) + _dot(tril, p3)
        gcol_ref[0, c * L:(c + 1) * L, :] = jnp.where(lane8_c < B_HEADS, cum, pc)
        pr = pre_r[:, c * L:(c + 1) * L]
        r1, r2, r3 = _split3(pr)
        cumr = _dot(r1, triu) + _dot(r2, triu) + _dot(r3, triu)
        grow_ref[0, c] = jnp.where(row8_c < B_HEADS, cumr, pr)


def _in_proj(x, norm_g, rope_t, conv_w, bcol, brow, wqka, wqkb, wo, wg, wif, wvt, wift):
    bn, s, d = x.shape
    tm = TM_IN
    nt = s // tm
    nc = tm // MOBA_BLOCK
    nblk = s // MOBA_BLOCK
    tok = lambda w: pl.BlockSpec((1, tm, w), lambda b, t: (b, t, 0))
    blk4 = lambda r: pl.BlockSpec((1, nc, r, MOBA_BLOCK), lambda b, t: (b, t, 0, 0))
    in_specs = [
        tok(d),
        _const_spec((1, d)),
        pl.BlockSpec((tm, ROPE_TABS * LANES), lambda b, t: (t, 0)),
        _const_spec(conv_w.shape), _const_spec(bcol.shape), _const_spec(brow.shape),
        _const_spec(wqka.shape), _const_spec(wqkb.shape), _const_spec(wo.shape),
        _const_spec(wg.shape), _const_spec(wif.shape), _const_spec(wvt.shape), _const_spec(wift.shape),
    ]
    out_shape = (
        jax.ShapeDtypeStruct((bn, s, A_WIDTH), BF16),
        jax.ShapeDtypeStruct((bn, s, A_WIDTH), BF16),
        jax.ShapeDtypeStruct((bn, nt, nc, A_WIDTH), F32),
        jax.ShapeDtypeStruct((bn, nblk, A_HEADS * VA_ROWS, MOBA_BLOCK), BF16),
        jax.ShapeDtypeStruct((bn, nblk, B_WIDTH, MOBA_BLOCK), BF16),
        jax.ShapeDtypeStruct((bn, s, B_WIDTH), BF16),
        jax.ShapeDtypeStruct((bn, s, B_WIDTH), BF16),
        jax.ShapeDtypeStruct((bn, s, B_WIDTH), BF16),
        jax.ShapeDtypeStruct((bn, s, D_MODEL), BF16),
        jax.ShapeDtypeStruct((bn, s, D_MODEL), BF16),
        jax.ShapeDtypeStruct((bn, nblk, 2 * B_HEADS, MOBA_BLOCK), F32),
        jax.ShapeDtypeStruct((bn, s, 2 * B_HEADS), F32),
    )
    out_specs = (
        tok(A_WIDTH), tok(A_WIDTH),
        pl.BlockSpec((1, 1, nc, A_WIDTH), lambda b, t: (b, t, 0, 0)),
        blk4(A_HEADS * VA_ROWS), blk4(B_WIDTH),
        tok(B_WIDTH), tok(B_WIDTH), tok(B_WIDTH),
        tok(D_MODEL), tok(D_MODEL),
        blk4(2 * B_HEADS),
        tok(2 * B_HEADS),
    )
    return pl.pallas_call(
        _inproj_kernel,
        grid=(bn, nt),
        in_specs=in_specs,
        out_specs=out_specs,
        out_shape=out_shape,
        scratch_shapes=[pltpu.VMEM((tm, d), BF16), pltpu.VMEM((tm + SUBLANES, 2 * B_WIDTH), F32)],
        compiler_params=pltpu.CompilerParams(
            dimension_semantics=("arbitrary", "arbitrary"), vmem_limit_bytes=VMEM_LIMIT_BYTES),
        name="in_proj",
    )(x, norm_g, rope_t, conv_w, bcol, brow, wqka, wqkb, wo, wg, wif, wvt, wift)


def _moba_kernel(q_ref, k_ref, vt_ref, kmean_ref, o_ref, sel_ref, acc_ref, mx_ref, s_ref):
    blk = MOBA_BLOCK
    nblk = kmean_ref.shape[1]
    hg = q_ref.shape[2] // A_HEAD_DIM
    i = pl.program_id(2)

    def pair_lanes(h):
        return slice((h // 2) * LANES, (h // 2 + 1) * LANES)

    def head_rows(h):
        return slice(h * VA_ROWS, (h + 1) * VA_ROWS)

    def group_max(st):
        return jnp.max(st.reshape(blk // SUBLANES, SUBLANES, blk), axis=0)

    @pl.when(i == 0)
    def _():
        mx_ref[...] = jnp.zeros_like(mx_ref)
        o_ref[...] = jnp.zeros_like(o_ref)

    for h in range(hg):
        acc_ref[h] = jnp.zeros(acc_ref.shape[1:], F32)

    def value_fn(prev):
        m_prev = [jnp.max(mx_ref[prev, h], axis=0, keepdims=True) for h in range(hg)]

        def value_tile(h, t):
            p = jnp.exp2(s_ref[prev, h, t] - m_prev[h]).astype(BF16)
            acc_ref[h] += _dot(vt_ref[0, t, head_rows(h), :], p)

        return value_tile

    def scores_and_values(cur):
        value_tile = value_fn(1 - cur)
        lane = lax.broadcasted_iota(jnp.int32, (blk, LANES), 1)
        lo_half = lane < A_HEAD_DIM
        qh = []
        for h in range(hg):
            qp = q_ref[0, :, pair_lanes(h)]
            qh.append(jnp.where(lo_half if h % 2 == 0 else jnp.logical_not(lo_half), qp, jnp.zeros_like(qp)))

        bidx = lax.broadcasted_iota(jnp.int32, (nblk, blk), 0)
        past = bidx < i
        for h in range(hg):
            km = kmean_ref[0, :, pair_lanes(h)]
            km_hi = km.astype(BF16)
            km_lo = (km - km_hi.astype(F32)).astype(BF16)
            gate = _dot_nt(km_hi, qh[h]) + _dot_nt(km_lo, qh[h])
            gate = jnp.where(past, gate, -jnp.inf)
            rank = jnp.zeros((nblk, blk), F32)
            for jp in range(nblk):
                gj = gate[jp:jp + 1, :]
                beats = jnp.where(gj > gate, 1.0, jnp.where((gj == gate) & (bidx > jp), 1.0, 0.0))
                rank = rank + beats
            sel_ref[h] = jnp.where(past & (rank < MOBA_TOPK), 1.0, 0.0)

        key_i = lax.broadcasted_iota(jnp.int32, (blk, blk), 0)
        qry_i = lax.broadcasted_iota(jnp.int32, (blk, blk), 1)
        causal = key_i <= qry_i
        d0 = pl.multiple_of(i * blk, blk)
        mx0 = []
        for h in range(hg):
            kd = k_ref[0, pl.ds(d0, blk), pair_lanes(h)]
            st = jnp.where(causal, _dot_nt(kd, qh[h]), -jnp.inf)
            s_ref[cur, h, i] = st
            mx0.append(group_max(st))

        def key_block(t, mx):
            r0 = pl.multiple_of(t * blk, blk)
            out = []
            for h in range(hg):
                kt = k_ref[0, pl.ds(r0, blk), pair_lanes(h)]
                st = jnp.where(sel_ref[h, pl.ds(t, 1), :] > 0.5, _dot_nt(kt, qh[h]), -jnp.inf)
                s_ref[cur, h, t] = st
                out.append(jnp.maximum(mx[h], group_max(st)))
                value_tile(h, t)
            return tuple(out)

        mx = lax.fori_loop(0, lax.shift_right_logical(i, 1),
                           lambda t2, mx: key_block(2 * t2 + 1, key_block(2 * t2, mx)), tuple(mx0))
        if cur == 1:
            mx = key_block(i - 1, mx)
        for h in range(hg):
            mx_ref[cur, h] = mx[h]

    for slot in range(2):
        pl.when((i < nblk) & (lax.rem(i, 2) == slot))(functools.partial(scores_and_values, slot))

    @pl.when(i == nblk)
    def _():
        value_tile = value_fn((nblk - 1) % 2)

        def body(t, _):
            for h in range(hg):
                value_tile(h, t)
            return 0

        lax.fori_loop(0, nblk, body, 0, unroll=2)

    @pl.when(i > 0)
    def _():
        outs = []
        for h in range(hg):
            acc = acc_ref[h]
            outs.append(acc[:A_HEAD_DIM] / acc[A_HEAD_DIM:A_HEAD_DIM + 1])
        o_ref[0] = jnp.concatenate(outs, axis=0).T.astype(BF16)


def _moba(qa, ka, vat, kmean):
    bn, s, _ = qa.shape
    blk = MOBA_BLOCK
    nblk = s // blk
    hg = MOBA_HEADS_PER_STEP
    ngroup = A_HEADS // hg
    gw = hg * A_HEAD_DIM
    return pl.pallas_call(
        _moba_kernel,
        grid=(bn, ngroup, nblk + 1),
        in_specs=[
            pl.BlockSpec((1, blk, gw), lambda b, g, i: (b, jnp.minimum(i, nblk - 1), g)),
            pl.BlockSpec((1, s, gw), lambda b, g, i: (b, 0, g)),
            pl.BlockSpec((1, nblk, hg * VA_ROWS, blk), lambda b, g, i: (b, 0, g, 0)),
            pl.BlockSpec((1, nblk, gw), lambda b, g, i: (b, 0, g)),
        ],
        out_specs=pl.BlockSpec((1, blk, gw), lambda b, g, i: (b, jnp.maximum(i - 1, 0), g)),
        out_shape=jax.ShapeDtypeStruct((bn, s, A_WIDTH), BF16),
        scratch_shapes=[pltpu.VMEM((hg, nblk, blk), F32),
                        pltpu.VMEM((hg, VA_ROWS, blk), F32),
                        pltpu.VMEM((2, hg, SUBLANES, blk), F32),
                        pltpu.VMEM((2, hg, nblk, blk, blk), F32)],
        compiler_params=pltpu.CompilerParams(
            dimension_semantics=("arbitrary", "arbitrary", "arbitrary"), vmem_limit_bytes=VMEM_LIMIT_BYTES),
        name="moba_attention",
    )(qa, ka, vat, kmean)


def _mlstm_kernel(q_ref, k_ref, vt_ref, o_ref, grow_ref, gcol_ref, ng_ref, y_ref, ctn_ref, m_ref):
    L = MLSTM_L
    d = B_HEAD_DIM
    nchunk = q_ref.shape[1] // L

    @pl.when(pl.program_id(1) == 0)
    def _():
        ctn_ref[...] = jnp.zeros_like(ctn_ref)
        m_ref[...] = jnp.zeros_like(m_ref)

    s_i = lax.broadcasted_iota(jnp.int32, (L, L), 0)
    t_i = lax.broadcasted_iota(jnp.int32, (L, L), 1)
    causal = s_i <= t_i
    aug_row = lax.broadcasted_iota(jnp.int32, (BF16_SUBLANE_PACK, L), 0)
    ones_blk = jnp.where(aug_row == 0, 1.0, 0.0)

    def chunk(c, _):
        r0 = pl.multiple_of(c * L, L)
        grow = grow_ref[0, c]
        gcol = gcol_ref[0, pl.ds(r0, L), :]
        for h in range(B_HEADS):
            hs = slice(h * d, (h + 1) * d)
            qc = q_ref[0, pl.ds(r0, L), hs]
            kc = k_ref[0, pl.ds(r0, L), hs]
            vt = vt_ref[0, c, hs, :].astype(F32)
            b_row = grow[h:h + 1, :]
            i_row = grow[B_HEADS + h:B_HEADS + h + 1, :]
            r_col = gcol[:, B_HEADS + h:B_HEADS + h + 1] - gcol[:, h:h + 1]
            m_prev = m_ref[h, 0:1, :]

            dm = jnp.where(causal, r_col + b_row, -jnp.inf)
            m_inter = b_row + m_prev
            m_t = jnp.maximum(m_inter, jnp.max(dm, axis=0, keepdims=True))
            pw = (_dot_nt(kc, qc) * jnp.exp(dm - m_t)).astype(BF16)
            w_inter = jnp.exp(m_inter - m_t)
            vt_aug = jnp.concatenate([vt, ones_blk], axis=0).astype(BF16)
            ctn = ctn_ref[h]
            tot = _dot(vt_aug, pw) + w_inter * _dot_nt(ctn.astype(BF16), qc)
            den = tot[d:d + 1, :]
            ht = tot[:d, :] / jnp.maximum(jnp.abs(den), jnp.exp(-m_t))
            ms = jnp.mean(ht * ht, axis=0, keepdims=True)
            hn = (ht * lax.rsqrt(ms + NORM_EPS)).T
            og = jax.nn.sigmoid(o_ref[0, pl.ds(r0, L), hs].astype(F32))
            y_ref[0, pl.ds(r0, L), hs] = (og * (hn * ng_ref[:, hs])).astype(BF16)

            b_tot = b_row[:, L - 1:L]
            g = b_tot - b_row + i_row
            m_new = jnp.maximum(b_tot + m_prev, jnp.max(g, axis=1, keepdims=True))
            w_c = jnp.exp(b_tot + m_prev - m_new)
            w_s = jnp.exp(g - m_new)
            vw = jnp.concatenate([vt * w_s, ones_blk * w_s], axis=0).astype(BF16)
            ctn_ref[h] = w_c[:, :d] * ctn + _dot(vw, kc)
            m_ref[h] = jnp.broadcast_to(m_new, m_ref.shape[1:])
        return 0

    lax.fori_loop(0, nchunk, chunk, 0)


def _mlstm(qb, kb, vbt, ob, grow, gcol, norm_g):
    bn, s, w = qb.shape
    ts = TS_MLSTM
    nc = ts // MLSTM_L
    tok = pl.BlockSpec((1, ts, w), lambda b, t: (b, t, 0))
    return pl.pallas_call(
        _mlstm_kernel,
        grid=(bn, s // ts),
        in_specs=[
            tok, tok,
            pl.BlockSpec((1, nc, w, MLSTM_L), lambda b, t: (b, t, 0, 0)),
            tok,
            pl.BlockSpec((1, nc, 2 * B_HEADS, MLSTM_L), lambda b, t: (b, t, 0, 0)),
            pl.BlockSpec((1, ts, 2 * B_HEADS), lambda b, t: (b, t, 0)),
            _const_spec((1, w)),
        ],
        out_specs=tok,
        out_shape=jax.ShapeDtypeStruct((bn, s, w), BF16),
        scratch_shapes=[pltpu.VMEM((B_HEADS, AUG_ROWS, B_HEAD_DIM), F32),
                        pltpu.VMEM((B_HEADS, SUBLANES, MLSTM_L), F32)],
        compiler_params=pltpu.CompilerParams(
            dimension_semantics=("arbitrary", "arbitrary"), vmem_limit_bytes=VMEM_LIMIT_BYTES),
        name="mlstm_chunkwise",
    )(qb, kb, vbt, ob, grow, gcol, norm_g)


def _merge_ffn_kernel(x_ref, ya_ref, yb_ref, sga_ref, sgb_ref, wpa_ref, wpb_ref, wout_ref,
                      gffn_ref, wgu_ref, wdown_ref, gfin_ref, o_ref, u_ref, h_ref, act_ref):
    merged = (sga_ref[0].astype(F32) * _dot(ya_ref[0], wpa_ref[...])
              + sgb_ref[0].astype(F32) * _dot(yb_ref[0], wpb_ref[...]))
    h = x_ref[0] + _dot(merged.astype(BF16), wout_ref[...])
    var = jnp.mean(h * h, axis=-1, keepdims=True)
    u_ref[...] = (h * lax.rsqrt(var + NORM_EPS) * gffn_ref[...]).astype(BF16)
    h_ref[...] = h

    for c in range(wgu_ref.shape[0]):
        gu = _dot(u_ref[...], wgu_ref[c])
        g = gu[:, :FF_CHUNK]
        act_ref[:, c * FF_CHUNK:(c + 1) * FF_CHUNK] = (g * jax.nn.sigmoid(g) * gu[:, FF_CHUNK:]).astype(BF16)

    h2 = h_ref[...] + _dot(act_ref[...], wdown_ref[...])
    var2 = jnp.mean(h2 * h2, axis=-1, keepdims=True)
    o_ref[0] = h2 * lax.rsqrt(var2 + NORM_EPS) * gfin_ref[...]


def _merge_ffn(x, ya, yb, sga, sgb, wpa, wpb, wout, gffn, wgu, wdown, gfin):
    bn, s, d = x.shape
    tm = TM_OUT
    tok = lambda w: pl.BlockSpec((1, tm, w), lambda b, t: (b, t, 0))
    return pl.pallas_call(
        _merge_ffn_kernel,
        grid=(bn, s // tm),
        in_specs=[tok(d), tok(A_WIDTH), tok(B_WIDTH), tok(d), tok(d),
                  _const_spec(wpa.shape), _const_spec(wpb.shape), _const_spec(wout.shape),
                  _const_spec(gffn.shape), _const_spec(wgu.shape),
                  _const_spec(wdown.shape), _const_spec(gfin.shape)],
        out_specs=tok(d),
        out_shape=jax.ShapeDtypeStruct((bn, s, d), F32),
        scratch_shapes=[pltpu.VMEM((tm, d), BF16), pltpu.VMEM((tm, d), F32), pltpu.VMEM((tm, D_FF), BF16)],
        compiler_params=pltpu.CompilerParams(
            dimension_semantics=("arbitrary", "arbitrary"), vmem_limit_bytes=VMEM_LIMIT_BYTES),
        name="merge_ffn",
    )(x, ya, yb, sga, sgb, wpa, wpb, wout, gffn, wgu, wdown, gfin)


def kernel(x, norm_mix_g, w_in, conv_w, b_igate, b_fgate, mlstm_norm_g, w_proj_a, w_proj_b,
           w_out, norm_ffn_g, w_gate_up, w_down, norm_final_g):
    bn, s, d = x.shape
    assert d == D_MODEL and s % TS_MLSTM == 0 and s % TM_IN == 0 and s % TM_OUT == 0
    assert norm_mix_g.shape[0] == 1, "single layer"
    w = w_in[0]
    o_qa, o_ka, o_va = 0, A_WIDTH, 2 * A_WIDTH
    o_qkb = 3 * A_WIDTH
    o_vb = o_qkb + 2 * B_WIDTH
    o_ob = o_vb + B_WIDTH
    o_i = o_ob + B_WIDTH
    o_f = o_i + B_HEADS
    o_ga = o_f + B_HEADS
    o_gb = o_ga + D_MODEL

    wqka = w[:, o_qa:o_va].astype(BF16)
    wqkb = w[:, o_qkb:o_vb].astype(BF16)
    wo = w[:, o_ob:o_i].astype(BF16)
    wg = w[:, o_ga:o_gb + D_MODEL].astype(BF16)
    w_fi = jnp.concatenate([w[:, o_f:o_f + B_HEADS], w[:, o_i:o_i + B_HEADS]], axis=1)
    wif = w_fi.astype(BF16)
    wift = w_fi.T.astype(BF16)
    wvt = jnp.concatenate([w[:, o_va:o_qkb], w[:, o_vb:o_ob]], axis=1).T.astype(BF16)
    bias_fi = jnp.concatenate([b_fgate[0], b_igate[0]]).astype(F32)
    bcol = bias_fi[None, :]
    brow = bias_fi[:, None]

    rope_t = _rope_tables(s)
    (qa, ka, kmean, vat, vbt, qb, kb, ob, sga, sgb, grow, gcol) = _in_proj(
        x, norm_mix_g, rope_t, conv_w[0], bcol, brow, wqka, wqkb, wo, wg, wif, wvt, wift)
    kmean = kmean.reshape(bn, s // MOBA_BLOCK, A_WIDTH)

    ya = _moba(qa, ka, vat, kmean)
    yb = _mlstm(qb, kb, vbt, ob, grow, gcol, mlstm_norm_g)

    nff = D_FF // FF_CHUNK
    wgu = w_gate_up[0].astype(BF16).reshape(d, 2, nff, FF_CHUNK).transpose(2, 0, 1, 3).reshape(nff, d, 2 * FF_CHUNK)
    return _merge_ffn(x, ya, yb, sga, sgb,
                      w_proj_a[0].astype(BF16), w_proj_b[0].astype(BF16), w_out[0].astype(BF16),
                      norm_ffn_g, wgu, w_down[0].astype(BF16), norm_final_g[None, :])
```

```python
import functools

import jax
import jax.numpy as jnp
import numpy as np
from jax import lax
from jax.experimental import pallas as pl
from jax.experimental.pallas import tpu as pltpu

F32 = jnp.float32
BF16 = jnp.bfloat16

D_MODEL = 1024
A_HEADS = 8
A_HEAD_DIM = 64
A_WIDTH = A_HEADS * A_HEAD_DIM
MOBA_BLOCK = 256
MOBA_TOPK = 3
ROPE_THETA = 10000.0
B_HEADS = 4
B_HEAD_DIM = 128
B_WIDTH = B_HEADS * B_HEAD_DIM
CONV_WIDTH = 4
D_FF = 2816
NORM_EPS = 1e-6

LANES = 128
SUBLANES = 8
BF16_SUBLANE_PACK = 16
VMEM_LIMIT_BYTES = 56 * 1024 * 1024

TM_IN = 512
MLSTM_L = 256
TS_MLSTM = 1024
TM_OUT = 512
FF_CHUNK = 256
IN_CHUNK = 256
AUG_ROWS = B_HEAD_DIM + BF16_SUBLANE_PACK
MOBA_HEADS_PER_STEP = 4
VA_ROWS = A_HEAD_DIM + BF16_SUBLANE_PACK
LOG2E = 1.4426950408889634
Q_SCALE = A_HEAD_DIM ** -0.5 * LOG2E
ROPE_TABS = 6


def _dot(a, b):
    return jnp.dot(a, b, preferred_element_type=F32)


def _dot_nt(a, b):
    return lax.dot_general(a, b, (((1,), (1,)), ((), ())), preferred_element_type=F32)


def _split3(x):
    x1 = x.astype(BF16)
    r1 = x - x1.astype(F32)
    x2 = r1.astype(BF16)
    r2 = r1 - x2.astype(F32)
    return x1, x2, r2.astype(BF16)


def _const_spec(shape):
    nd = len(shape)
    return pl.BlockSpec(shape, lambda *_: (0,) * nd, pipeline_mode=pl.Buffered(1))


def _rope_table_kernel(inv_ref, tab_ref):
    s = tab_ref.shape[0]
    pos = lax.broadcasted_iota(jnp.int32, (s, LANES), 0).astype(F32)
    lane = lax.broadcasted_iota(jnp.int32, (s, LANES), 1)
    ang = pos * inv_ref[...]
    first_half = (lane % A_HEAD_DIM) < (A_HEAD_DIM // 2)
    cs = jnp.cos(ang)
    sn = jnp.sin(ang)
    parts = [cs, jnp.where(first_half, -sn, 0.0), jnp.where(first_half, 0.0, sn)]
    for i, p in enumerate(parts):
        tab_ref[:, i * LANES:(i + 1) * LANES] = p
        tab_ref[:, (i + 3) * LANES:(i + 4) * LANES] = p * Q_SCALE


def _rope_tables(s):
    half = A_HEAD_DIM // 2
    inv = ROPE_THETA ** (-jnp.arange(half, dtype=F32) / half)
    inv = jnp.tile(inv, LANES // half)[None, :]
    return pl.pallas_call(
        _rope_table_kernel,
        out_shape=jax.ShapeDtypeStruct((s, ROPE_TABS * LANES), F32),
        name="rope_tables",
    )(inv)


def _log_sigmoid(x):
    return jnp.minimum(x, 0.0) - jnp.log(1.0 + jnp.exp(-jnp.abs(x)))


def _inproj_kernel(x_ref, g_ref, rope_ref, convw_ref, bcol_ref, brow_ref,
                   wqka_ref, wqkb_ref, wo_ref, wg_ref, wif_ref, wvt_ref, wift_ref,
                   qa_ref, ka_ref, kmean_ref, vat_ref, vbt_ref, qb_ref, kb_ref, ob_ref,
                   sga_ref, sgb_ref, grow_ref, gcol_ref,
                   u_ref, zs_ref):
    tm = x_ref.shape[1]
    nchunk = tm // MOBA_BLOCK
    t_idx = pl.program_id(1)

    xf = x_ref[0]
    var = jnp.mean(xf * xf, axis=-1, keepdims=True)
    u_ref[...] = (xf * lax.rsqrt(var + NORM_EPS) * g_ref[...]).astype(BF16)
    u = u_ref[...]

    half = A_HEAD_DIM // 2

    NW = IN_CHUNK

    def rope_store(z, tab0, out_ref, col0, with_mean):
        c, s_lo, s_hi = [rope_ref[:, (tab0 + i) * LANES:(tab0 + i + 1) * LANES] for i in range(3)]
        for lt in range(NW // LANES):
            zt = z[:, lt * LANES:(lt + 1) * LANES]
            rt = zt * c + pltpu.roll(zt, LANES - half, 1) * s_lo + pltpu.roll(zt, half, 1) * s_hi
            ls = slice(col0 + lt * LANES, col0 + (lt + 1) * LANES)
            out_ref[0, :, ls] = rt.astype(BF16)
            if with_mean:
                for cb in range(nchunk):
                    kmean_ref[0, 0, cb:cb + 1, ls] = jnp.mean(rt[cb * MOBA_BLOCK:(cb + 1) * MOBA_BLOCK],
                                                             axis=0, keepdims=True)

    def q_piece(p):
        rope_store(_dot(u_ref[...], wqka_ref[:, p * NW:(p + 1) * NW]), 3, qa_ref, p * NW, False)

    def k_piece(p):
        rope_store(_dot(u_ref[...], wqka_ref[:, A_WIDTH + p * NW:A_WIDTH + (p + 1) * NW]), 0, ka_ref, p * NW, True)

    ones_row = lax.broadcasted_iota(jnp.int32, (BF16_SUBLANE_PACK, MOBA_BLOCK), 0) == 0
    ones_blk = jnp.where(ones_row, 1.0, 0.0).astype(BF16)
    heads_per_piece = NW // A_HEAD_DIM

    def va_piece(p):
        vat = _dot_nt(wvt_ref[p * NW:(p + 1) * NW, :], u_ref[...])
        for c in range(nchunk):
            cs = slice(c * MOBA_BLOCK, (c + 1) * MOBA_BLOCK)
            for hh in range(heads_per_piece):
                h = p * heads_per_piece + hh
                piece = vat[hh * A_HEAD_DIM:(hh + 1) * A_HEAD_DIM, cs].astype(BF16)
                vat_ref[0, c, h * VA_ROWS:h * VA_ROWS + A_HEAD_DIM, :] = piece
                vat_ref[0, c, h * VA_ROWS + A_HEAD_DIM:(h + 1) * VA_ROWS, :] = ones_blk

    def vb_piece(p):
        vbt = _dot_nt(wvt_ref[A_WIDTH + p * NW:A_WIDTH + (p + 1) * NW, :], u_ref[...])
        for c in range(nchunk):
            vbt_ref[0, c, p * NW:(p + 1) * NW, :] = vbt[:, c * MOBA_BLOCK:(c + 1) * MOBA_BLOCK].astype(BF16)

    @pl.when(t_idx == 0)
    def _():
        zs_ref[:SUBLANES, :] = jnp.zeros((SUBLANES, zs_ref.shape[1]), F32)

    def conv_piece(p):
        ps = slice(p * NW, (p + 1) * NW)
        zs_ref[SUBLANES:, ps] = _dot(u_ref[...], wqkb_ref[:, ps])
        conv = None
        for j in range(CONV_WIDTH):
            off = SUBLANES - (CONV_WIDTH - 1) + j
            term = zs_ref[off:off + tm, ps] * convw_ref[j:j + 1, ps]
            conv = term if conv is None else conv + term
        zs_ref[:SUBLANES, ps] = zs_ref[tm:, ps]
        act = conv * jax.nn.sigmoid(conv)
        if p < B_WIDTH // NW:
            qb_ref[0, :, ps] = act.astype(BF16)
        else:
            kb_ref[0, :, p * NW - B_WIDTH:(p + 1) * NW - B_WIDTH] = (act * (B_HEAD_DIM ** -0.5)).astype(BF16)

    def o_piece(p):
        ps = slice(p * NW, (p + 1) * NW)
        ob_ref[0, :, ps] = _dot(u_ref[...], wo_ref[:, ps]).astype(BF16)

    def gate_piece(p):
        ps = slice(p * NW, (p + 1) * NW)
        sga_ref[0, :, ps] = _dot(u_ref[...], wg_ref[:, ps]).astype(BF16)
        sgb_ref[0, :, ps] = _dot(u_ref[...], wg_ref[:, D_MODEL + p * NW:D_MODEL + (p + 1) * NW]).astype(BF16)

    assert A_WIDTH // NW == 2 and B_WIDTH // NW == 2 and D_MODEL // NW == 4
    q_piece(0); va_piece(0)
    q_piece(1); va_piece(1)
    k_piece(0); vb_piece(0)
    k_piece(1); vb_piece(1)
    conv_piece(0); gate_piece(0)
    conv_piece(1); gate_piece(1)
    conv_piece(2); gate_piece(2); o_piece(0)
    conv_piece(3); gate_piece(3); o_piece(1)

    L = MLSTM_L
    zc = _dot(u, wif_ref[...]) + bcol_ref[...]
    lane8 = lax.broadcasted_iota(jnp.int32, zc.shape, 1)
    pre_c = jnp.where(lane8 < B_HEADS, _log_sigmoid(zc), zc)
    zr = _dot_nt(wift_ref[...], u) + brow_ref[...]
    row8 = lax.broadcasted_iota(jnp.int32, zr.shape, 0)
    pre_r = jnp.where(row8 < B_HEADS, _log_sigmoid(zr), zr)
    ti = lax.broadcasted_iota(jnp.int32, (L, L), 0)
    tj = lax.broadcasted_iota(jnp.int32, (L, L), 1)
    tril = jnp.where(tj <= ti, 1.0, 0.0).astype(BF16)
    triu = jnp.where(ti <= tj, 1.0, 0.0).astype(BF16)
    lane8_c = lax.broadcasted_iota(jnp.int32, (L, 2 * B_HEADS), 1)
    row8_c = lax.broadcasted_iota(jnp.int32, (2 * B_HEADS, L), 0)
    for c in range(tm // L):
        pc = pre_c[c * L:(c + 1) * L, :]
        p1, p2, p3 = _split3(pc)
        cum = _dot(tril, p1) + _dot(tril, p2) + _dot(tril, p3)
        gcol_ref[0, c * L:(c + 1) * L, :] = jnp.where(lane8_c < B_HEADS, cum, pc)
        pr = pre_r[:, c * L:(c + 1) * L]
        r1, r2, r3 = _split3(pr)
        cumr = _dot(r1, triu) + _dot(r2, triu) + _dot(r3, triu)
        grow_ref[0, c] = jnp.where(row8_c < B_HEADS, cumr, pr)


def _in_proj(x, norm_g, rope_t, conv_w, bcol, brow, wqka, wqkb, wo, wg, wif, wvt, wift):
    bn, s, d = x.shape
    tm = TM_IN
    nt = s // tm
    nc = tm // MOBA_BLOCK
    nblk = s // MOBA_BLOCK
    tok = lambda w: pl.BlockSpec((1, tm, w), lambda b, t: (b, t, 0))
    blk4 = lambda r: pl.BlockSpec((1, nc, r, MOBA_BLOCK), lambda b, t: (b, t, 0, 0))
    in_specs = [
        tok(d),
        _const_spec((1, d)),
        pl.BlockSpec((tm, ROPE_TABS * LANES), lambda b, t: (t, 0)),
        _const_spec(conv_w.shape), _const_spec(bcol.shape), _const_spec(brow.shape),
        _const_spec(wqka.shape), _const_spec(wqkb.shape), _const_spec(wo.shape),
        _const_spec(wg.shape), _const_spec(wif.shape), _const_spec(wvt.shape), _const_spec(wift.shape),
    ]
    out_shape = (
        jax.ShapeDtypeStruct((bn, s, A_WIDTH), BF16),
        jax.ShapeDtypeStruct((bn, s, A_WIDTH), BF16),
        jax.ShapeDtypeStruct((bn, nt, nc, A_WIDTH), F32),
        jax.ShapeDtypeStruct((bn, nblk, A_HEADS * VA_ROWS, MOBA_BLOCK), BF16),
        jax.ShapeDtypeStruct((bn, nblk, B_WIDTH, MOBA_BLOCK), BF16),
        jax.ShapeDtypeStruct((bn, s, B_WIDTH), BF16),
        jax.ShapeDtypeStruct((bn, s, B_WIDTH), BF16),
        jax.ShapeDtypeStruct((bn, s, B_WIDTH), BF16),
        jax.ShapeDtypeStruct((bn, s, D_MODEL), BF16),
        jax.ShapeDtypeStruct((bn, s, D_MODEL), BF16),
        jax.ShapeDtypeStruct((bn, nblk, 2 * B_HEADS, MOBA_BLOCK), F32),
        jax.ShapeDtypeStruct((bn, s, 2 * B_HEADS), F32),
    )
    out_specs = (
        tok(A_WIDTH), tok(A_WIDTH),
        pl.BlockSpec((1, 1, nc, A_WIDTH), lambda b, t: (b, t, 0, 0)),
        blk4(A_HEADS * VA_ROWS), blk4(B_WIDTH),
        tok(B_WIDTH), tok(B_WIDTH), tok(B_WIDTH),
        tok(D_MODEL), tok(D_MODEL),
        blk4(2 * B_HEADS),
        tok(2 * B_HEADS),
    )
    return pl.pallas_call(
        _inproj_kernel,
        grid=(bn, nt),
        in_specs=in_specs,
        out_specs=out_specs,
        out_shape=out_shape,
        scratch_shapes=[pltpu.VMEM((tm, d), BF16), pltpu.VMEM((tm + SUBLANES, 2 * B_WIDTH), F32)],
        compiler_params=pltpu.CompilerParams(
            dimension_semantics=("arbitrary", "arbitrary"), vmem_limit_bytes=VMEM_LIMIT_BYTES),
        name="in_proj",
    )(x, norm_g, rope_t, conv_w, bcol, brow, wqka, wqkb, wo, wg, wif, wvt, wift)


def _moba_kernel(q_ref, k_ref, vt_ref, kmean_ref, o_ref, sel_ref, acc_ref, mx_ref, s_ref):
    blk = MOBA_BLOCK
    nblk = kmean_ref.shape[1]
    hg = q_ref.shape[2] // A_HEAD_DIM
    i = pl.program_id(2)

    def pair_lanes(h):
        return slice((h // 2) * LANES, (h // 2 + 1) * LANES)

    def head_rows(h):
        return slice(h * VA_ROWS, (h + 1) * VA_ROWS)

    def group_max(st):
        return jnp.max(st.reshape(blk // SUBLANES, SUBLANES, blk), axis=0)

    @pl.when(i == 0)
    def _():
        mx_ref[...] = jnp.zeros_like(mx_ref)
        o_ref[...] = jnp.zeros_like(o_ref)

    for h in range(hg):
        acc_ref[h] = jnp.zeros(acc_ref.shape[1:], F32)

    def value_fn(prev):
        m_prev = [jnp.max(mx_ref[prev, h], axis=0, keepdims=True) for h in range(hg)]

        def value_tile(h, t):
            p = jnp.exp2(s_ref[prev, h, t] - m_prev[h]).astype(BF16)
            acc_ref[h] += _dot(vt_ref[0, t, head_rows(h), :], p)

        return value_tile

    def scores_and_values(cur):
        value_tile = value_fn(1 - cur)
        lane = lax.broadcasted_iota(jnp.int32, (blk, LANES), 1)
        lo_half = lane < A_HEAD_DIM
        qh = []
        for h in range(hg):
            qp = q_ref[0, :, pair_lanes(h)]
            qh.append(jnp.where(lo_half if h % 2 == 0 else jnp.logical_not(lo_half), qp, jnp.zeros_like(qp)))

        bidx = lax.broadcasted_iota(jnp.int32, (nblk, blk), 0)
        past = bidx < i
        later = [bidx > jp for jp in range(nblk)]
        for h in range(hg):
            km = kmean_ref[0, :, pair_lanes(h)]
            km_hi = km.astype(BF16)
            km_lo = (km - km_hi.astype(F32)).astype(BF16)
            gate = _dot_nt(km_hi, qh[h]) + _dot_nt(km_lo, qh[h])
            gate = jnp.where(past, gate, -jnp.inf)
            beats = []
            for jp in range(nblk):
                gj = gate[jp:jp + 1, :]
                beats.append(jnp.where(later[jp], jnp.where(gj >= gate, 1.0, 0.0), jnp.where(gj > gate, 1.0, 0.0)))
            while len(beats) > 1:
                beats = [a + b for a, b in zip(beats[::2], beats[1::2])]
            sel_ref[h] = jnp.where(past & (beats[0] < MOBA_TOPK), 1.0, 0.0)

        key_i = lax.broadcasted_iota(jnp.int32, (blk, blk), 0)
        qry_i = lax.broadcasted_iota(jnp.int32, (blk, blk), 1)
        causal = key_i <= qry_i
        d0 = pl.multiple_of(i * blk, blk)
        mx0 = []
        for h in range(hg):
            kd = k_ref[0, pl.ds(d0, blk), pair_lanes(h)]
            st = jnp.where(causal, _dot_nt(kd, qh[h]), -jnp.inf)
            s_ref[cur, h, i] = st
            mx0.append(group_max(st))

        def key_block(t, mx):
            r0 = pl.multiple_of(t * blk, blk)
            out = []
            for h in range(hg):
                kt = k_ref[0, pl.ds(r0, blk), pair_lanes(h)]
                st = jnp.where(sel_ref[h, pl.ds(t, 1), :] > 0.5, _dot_nt(kt, qh[h]), -jnp.inf)
                s_ref[cur, h, t] = st
                out.append(jnp.maximum(mx[h], group_max(st)))
                value_tile(h, t)
            return tuple(out)

        mx = lax.fori_loop(0, lax.shift_right_logical(i, 1),
                           lambda t2, mx: key_block(2 * t2 + 1, key_block(2 * t2, mx)), tuple(mx0))
        if cur == 1:
            mx = key_block(i - 1, mx)
        for h in range(hg):
            mx_ref[cur, h] = mx[h]

    for slot in range(2):
        pl.when((i < nblk) & (lax.rem(i, 2) == slot))(functools.partial(scores_and_values, slot))

    @pl.when(i == nblk)
    def _():
        value_tile = value_fn((nblk - 1) % 2)

        def body(t, _):
            for h in range(hg):
                value_tile(h, t)
            return 0

        lax.fori_loop(0, nblk, body, 0, unroll=2)

    @pl.when(i > 0)
    def _():
        outs = []
        for h in range(hg):
            acc = acc_ref[h]
            outs.append(acc[:A_HEAD_DIM] / acc[A_HEAD_DIM:A_HEAD_DIM + 1])
        o_ref[0] = jnp.concatenate(outs, axis=0).T.astype(BF16)


def _moba(qa, ka, vat, kmean):
    bn, s, _ = qa.shape
    blk = MOBA_BLOCK
    nblk = s // blk
    hg = MOBA_HEADS_PER_STEP
    ngroup = A_HEADS // hg
    gw = hg * A_HEAD_DIM
    return pl.pallas_call(
        _moba_kernel,
        grid=(bn, ngroup, nblk + 1),
        in_specs=[
            pl.BlockSpec((1, blk, gw), lambda b, g, i: (b, jnp.minimum(i, nblk - 1), g)),
            pl.BlockSpec((1, s, gw), lambda b, g, i: (b, 0, g)),
            pl.BlockSpec((1, nblk, hg * VA_ROWS, blk), lambda b, g, i: (b, 0, g, 0)),
            pl.BlockSpec((1, nblk, gw), lambda b, g, i: (b, 0, g)),
        ],
        out_specs=pl.BlockSpec((1, blk, gw), lambda b, g, i: (b, jnp.maximum(i - 1, 0), g)),
        out_shape=jax.ShapeDtypeStruct((bn, s, A_WIDTH), BF16),
        scratch_shapes=[pltpu.VMEM((hg, nblk, blk), F32),
                        pltpu.VMEM((hg, VA_ROWS, blk), F32),
                        pltpu.VMEM((2, hg, SUBLANES, blk), F32),
                        pltpu.VMEM((2, hg, nblk, blk, blk), F32)],
        compiler_params=pltpu.CompilerParams(
            dimension_semantics=("arbitrary", "arbitrary", "arbitrary"), vmem_limit_bytes=VMEM_LIMIT_BYTES),
        name="moba_attention",
    )(qa, ka, vat, kmean)


def _mlstm_kernel(q_ref, k_ref, vt_ref, o_ref, grow_ref, gcol_ref, ng_ref, y_ref, ctn_ref, m_ref):
    L = MLSTM_L
    d = B_HEAD_DIM
    nchunk = q_ref.shape[1] // L

    @pl.when(pl.program_id(1) == 0)
    def _():
        ctn_ref[...] = jnp.zeros_like(ctn_ref)
        m_ref[...] = jnp.zeros_like(m_ref)

    s_i = lax.broadcasted_iota(jnp.int32, (L, L), 0)
    t_i = lax.broadcasted_iota(jnp.int32, (L, L), 1)
    causal = s_i <= t_i
    aug_row = lax.broadcasted_iota(jnp.int32, (BF16_SUBLANE_PACK, L), 0)
    ones_blk = jnp.where(aug_row == 0, 1.0, 0.0)

    def chunk(c, _):
        r0 = pl.multiple_of(c * L, L)
        grow = grow_ref[0, c]
        gcol = gcol_ref[0, pl.ds(r0, L), :]
        for h in range(B_HEADS):
            hs = slice(h * d, (h + 1) * d)
            qc = q_ref[0, pl.ds(r0, L), hs]
            kc = k_ref[0, pl.ds(r0, L), hs]
            vt = vt_ref[0, c, hs, :].astype(F32)
            b_row = grow[h:h + 1, :]
            i_row = grow[B_HEADS + h:B_HEADS + h + 1, :]
            r_col = gcol[:, B_HEADS + h:B_HEADS + h + 1] - gcol[:, h:h + 1]
            m_prev = m_ref[h, 0:1, :]

            dm = jnp.where(causal, r_col + b_row, -jnp.inf)
            m_inter = b_row + m_prev
            m_t = jnp.maximum(m_inter, jnp.max(dm, axis=0, keepdims=True))
            pw = (_dot_nt(kc, qc) * jnp.exp(dm - m_t)).astype(BF16)
            w_inter = jnp.exp(m_inter - m_t)
            vt_aug = jnp.concatenate([vt, ones_blk], axis=0).astype(BF16)
            ctn = ctn_ref[h]
            tot = _dot(vt_aug, pw) + w_inter * _dot_nt(ctn.astype(BF16), qc)
            den = tot[d:d + 1, :]
            ht = tot[:d, :] / jnp.maximum(jnp.abs(den), jnp.exp(-m_t))
            ms = jnp.mean(ht * ht, axis=0, keepdims=True)
            hn = (ht * lax.rsqrt(ms + NORM_EPS)).T
            og = jax.nn.sigmoid(o_ref[0, pl.ds(r0, L), hs].astype(F32))
            y_ref[0, pl.ds(r0, L), hs] = (og * (hn * ng_ref[:, hs])).astype(BF16)

            b_tot = b_row[:, L - 1:L]
            g = b_tot - b_row + i_row
            m_new = jnp.maximum(b_tot + m_prev, jnp.max(g, axis=1, keepdims=True))
            w_c = jnp.exp(b_tot + m_prev - m_new)
            w_s = jnp.exp(g - m_new)
            vw = jnp.concatenate([vt * w_s, ones_blk * w_s], axis=0).astype(BF16)
            ctn_ref[h] = w_c[:, :d] * ctn + _dot(vw, kc)
            m_ref[h] = jnp.broadcast_to(m_new, m_ref.shape[1:])
        return 0

    lax.fori_loop(0, nchunk, chunk, 0)


def _mlstm(qb, kb, vbt, ob, grow, gcol, norm_g):
    bn, s, w = qb.shape
    ts = TS_MLSTM
    nc = ts // MLSTM_L
    tok = pl.BlockSpec((1, ts, w), lambda b, t: (b, t, 0))
    return pl.pallas_call(
        _mlstm_kernel,
        grid=(bn, s // ts),
        in_specs=[
            tok, tok,
            pl.BlockSpec((1, nc, w, MLSTM_L), lambda b, t: (b, t, 0, 0)),
            tok,
            pl.BlockSpec((1, nc, 2 * B_HEADS, MLSTM_L), lambda b, t: (b, t, 0, 0)),
            pl.BlockSpec((1, ts, 2 * B_HEADS), lambda b, t: (b, t, 0)),
            _const_spec((1, w)),
        ],
        out_specs=tok,
        out_shape=jax.ShapeDtypeStruct((bn, s, w), BF16),
        scratch_shapes=[pltpu.VMEM((B_HEADS, AUG_ROWS, B_HEAD_DIM), F32),
                        pltpu.VMEM((B_HEADS, SUBLANES, MLSTM_L), F32)],
        compiler_params=pltpu.CompilerParams(
            dimension_semantics=("arbitrary", "arbitrary"), vmem_limit_bytes=VMEM_LIMIT_BYTES),
        name="mlstm_chunkwise",
    )(qb, kb, vbt, ob, grow, gcol, norm_g)


def _merge_ffn_kernel(x_ref, ya_ref, yb_ref, sga_ref, sgb_ref, wpa_ref, wpb_ref, wout_ref,
                      gffn_ref, wgu_ref, wdown_ref, gfin_ref, o_ref, u_ref, h_ref, act_ref):
    merged = (jax.nn.sigmoid(sga_ref[0].astype(F32)) * _dot(ya_ref[0], wpa_ref[...])
              + jax.nn.sigmoid(sgb_ref[0].astype(F32)) * _dot(yb_ref[0], wpb_ref[...]))
    h = x_ref[0] + _dot(merged.astype(BF16), wout_ref[...])
    var = jnp.mean(h * h, axis=-1, keepdims=True)
    u_ref[...] = (h * lax.rsqrt(var + NORM_EPS) * gffn_ref[...]).astype(BF16)
    h_ref[...] = h

    for c in range(D_FF // FF_CHUNK):
        g = _dot(u_ref[...], wgu_ref[:, c * FF_CHUNK:(c + 1) * FF_CHUNK])
        up = _dot(u_ref[...], wgu_ref[:, D_FF + c * FF_CHUNK:D_FF + (c + 1) * FF_CHUNK])
        act_ref[:, c * FF_CHUNK:(c + 1) * FF_CHUNK] = (g * jax.nn.sigmoid(g) * up).astype(BF16)

    h2 = h_ref[...] + _dot(act_ref[...], wdown_ref[...])
    var2 = jnp.mean(h2 * h2, axis=-1, keepdims=True)
    o_ref[0] = h2 * lax.rsqrt(var2 + NORM_EPS) * gfin_ref[...]


def _merge_ffn(x, ya, yb, sga, sgb, wpa, wpb, wout, gffn, wgu, wdown, gfin):
    bn, s, d = x.shape
    tm = TM_OUT
    tok = lambda w: pl.BlockSpec((1, tm, w), lambda b, t: (b, t, 0))
    return pl.pallas_call(
        _merge_ffn_kernel,
        grid=(bn, s // tm),
        in_specs=[tok(d), tok(A_WIDTH), tok(B_WIDTH), tok(d), tok(d),
                  _const_spec(wpa.shape), _const_spec(wpb.shape), _const_spec(wout.shape),
                  _const_spec(gffn.shape), _const_spec(wgu.shape),
                  _const_spec(wdown.shape), _const_spec(gfin.shape)],
        out_specs=tok(d),
        out_shape=jax.ShapeDtypeStruct((bn, s, d), F32),
        scratch_shapes=[pltpu.VMEM((tm, d), BF16), pltpu.VMEM((tm, d), F32), pltpu.VMEM((tm, D_FF), BF16)],
        compiler_params=pltpu.CompilerParams(
            dimension_semantics=("arbitrary", "arbitrary"), vmem_limit_bytes=VMEM_LIMIT_BYTES),
        name="merge_ffn",
    )(x, ya, yb, sga, sgb, wpa, wpb, wout, gffn, wgu, wdown, gfin)


def kernel(x, norm_mix_g, w_in, conv_w, b_igate, b_fgate, mlstm_norm_g, w_proj_a, w_proj_b,
           w_out, norm_ffn_g, w_gate_up, w_down, norm_final_g):
    bn, s, d = x.shape
    assert d == D_MODEL and s % TS_MLSTM == 0 and s % TM_IN == 0 and s % TM_OUT == 0
    assert norm_mix_g.shape[0] == 1, "single layer"
    w = w_in[0]
    o_qa, o_ka, o_va = 0, A_WIDTH, 2 * A_WIDTH
    o_qkb = 3 * A_WIDTH
    o_vb = o_qkb + 2 * B_WIDTH
    o_ob = o_vb + B_WIDTH
    o_i = o_ob + B_WIDTH
    o_f = o_i + B_HEADS
    o_ga = o_f + B_HEADS
    o_gb = o_ga + D_MODEL

    wqka = w[:, o_qa:o_va].astype(BF16)
    wqkb = w[:, o_qkb:o_vb].astype(BF16)
    wo = w[:, o_ob:o_i].astype(BF16)
    wg = w[:, o_ga:o_gb + D_MODEL].astype(BF16)
    w_fi = jnp.concatenate([w[:, o_f:o_f + B_HEADS], w[:, o_i:o_i + B_HEADS]], axis=1)
    wif = w_fi.astype(BF16)
    wift = w_fi.T.astype(BF16)
    wvt = jnp.concatenate([w[:, o_va:o_qkb], w[:, o_vb:o_ob]], axis=1).T.astype(BF16)
    bias_fi = jnp.concatenate([b_fgate[0], b_igate[0]]).astype(F32)
    bcol = bias_fi[None, :]
    brow = bias_fi[:, None]

    rope_t = _rope_tables(s)
    (qa, ka, kmean, vat, vbt, qb, kb, ob, sga, sgb, grow, gcol) = _in_proj(
        x, norm_mix_g, rope_t, conv_w[0], bcol, brow, wqka, wqkb, wo, wg, wif, wvt, wift)
    kmean = kmean.reshape(bn, s // MOBA_BLOCK, A_WIDTH)

    ya = _moba(qa, ka, vat, kmean)
    yb = _mlstm(qb, kb, vbt, ob, grow, gcol, mlstm_norm_g)

    wgu = w_gate_up[0].astype(BF16)
    return _merge_ffn(x, ya, yb, sga, sgb,
                      w_proj_a[0].astype(BF16), w_proj_b[0].astype(BF16), w_out[0].astype(BF16),
                      norm_ffn_g, wgu, w_down[0].astype(BF16), norm_final_g[None, :])
```

```python
import functools

import jax
import jax.numpy as jnp
import numpy as np
from jax import lax
from jax.experimental import pallas as pl
from jax.experimental.pallas import tpu as pltpu

F32 = jnp.float32
BF16 = jnp.bfloat16

D_MODEL = 1024
A_HEADS = 8
A_HEAD_DIM = 64
A_WIDTH = A_HEADS * A_HEAD_DIM
MOBA_BLOCK = 256
MOBA_TOPK = 3
ROPE_THETA = 10000.0
B_HEADS = 4
B_HEAD_DIM = 128
B_WIDTH = B_HEADS * B_HEAD_DIM
CONV_WIDTH = 4
D_FF = 2816
NORM_EPS = 1e-6

LANES = 128
SUBLANES = 8
BF16_SUBLANE_PACK = 16
VMEM_LIMIT_BYTES = 56 * 1024 * 1024

TM_IN = 512
MLSTM_L = 256
TS_MLSTM = 1024
TM_OUT = 512
FF_CHUNK = 256
IN_CHUNK = 256
AUG_ROWS = B_HEAD_DIM + BF16_SUBLANE_PACK
MOBA_HEADS_PER_STEP = 8
VA_ROWS = A_HEAD_DIM + BF16_SUBLANE_PACK
LOG2E = 1.4426950408889634
Q_SCALE = A_HEAD_DIM ** -0.5 * LOG2E
ROPE_TABS = 6


def _dot(a, b):
    return jnp.dot(a, b, preferred_element_type=F32)


def _dot_nt(a, b):
    return lax.dot_general(a, b, (((1,), (1,)), ((), ())), preferred_element_type=F32)


def _split3(x):
    x1 = x.astype(BF16)
    r1 = x - x1.astype(F32)
    x2 = r1.astype(BF16)
    r2 = r1 - x2.astype(F32)
    return x1, x2, r2.astype(BF16)


def _const_spec(shape):
    nd = len(shape)
    return pl.BlockSpec(shape, lambda *_: (0,) * nd, pipeline_mode=pl.Buffered(1))


def _rope_table_kernel(inv_ref, tab_ref):
    s = tab_ref.shape[0]
    pos = lax.broadcasted_iota(jnp.int32, (s, LANES), 0).astype(F32)
    lane = lax.broadcasted_iota(jnp.int32, (s, LANES), 1)
    ang = pos * inv_ref[...]
    first_half = (lane % A_HEAD_DIM) < (A_HEAD_DIM // 2)
    cs = jnp.cos(ang)
    sn = jnp.sin(ang)
    parts = [cs, jnp.where(first_half, -sn, 0.0), jnp.where(first_half, 0.0, sn)]
    for i, p in enumerate(parts):
        tab_ref[:, i * LANES:(i + 1) * LANES] = p
        tab_ref[:, (i + 3) * LANES:(i + 4) * LANES] = p * Q_SCALE


def _rope_tables(s):
    half = A_HEAD_DIM // 2
    inv = ROPE_THETA ** (-jnp.arange(half, dtype=F32) / half)
    inv = jnp.tile(inv, LANES // half)[None, :]
    return pl.pallas_call(
        _rope_table_kernel,
        out_shape=jax.ShapeDtypeStruct((s, ROPE_TABS * LANES), F32),
        name="rope_tables",
    )(inv)


def _log_sigmoid(x):
    return jnp.minimum(x, 0.0) - jnp.log(1.0 + jnp.exp(-jnp.abs(x)))


def _inproj_kernel(x_ref, g_ref, rope_ref, convw_ref, bcol_ref, brow_ref,
                   wqka_ref, wqkb_ref, wo_ref, wg_ref, wif_ref, wvt_ref, wift_ref,
                   qa_ref, ka_ref, kmean_ref, vat_ref, vbt_ref, qb_ref, kb_ref, ob_ref,
                   sga_ref, sgb_ref, grow_ref, gcol_ref,
                   u_ref, zs_ref):
    tm = x_ref.shape[1]
    nchunk = tm // MOBA_BLOCK
    t_idx = pl.program_id(1)

    xf = x_ref[0]
    var = jnp.mean(xf * xf, axis=-1, keepdims=True)
    u_ref[...] = (xf * lax.rsqrt(var + NORM_EPS) * g_ref[...]).astype(BF16)
    u = u_ref[...]

    half = A_HEAD_DIM // 2

    NW = IN_CHUNK

    def rope_store(z, tab0, out_ref, col0, with_mean):
        c, s_lo, s_hi = [rope_ref[:, (tab0 + i) * LANES:(tab0 + i + 1) * LANES] for i in range(3)]
        for lt in range(NW // LANES):
            zt = z[:, lt * LANES:(lt + 1) * LANES]
            rt = zt * c + pltpu.roll(zt, LANES - half, 1) * s_lo + pltpu.roll(zt, half, 1) * s_hi
            ls = slice(col0 + lt * LANES, col0 + (lt + 1) * LANES)
            out_ref[0, :, ls] = rt.astype(BF16)
            if with_mean:
                for cb in range(nchunk):
                    kmean_ref[0, 0, cb:cb + 1, ls] = jnp.mean(rt[cb * MOBA_BLOCK:(cb + 1) * MOBA_BLOCK],
                                                             axis=0, keepdims=True)

    def q_piece(p):
        rope_store(_dot(u_ref[...], wqka_ref[:, p * NW:(p + 1) * NW]), 3, qa_ref, p * NW, False)

    def k_piece(p):
        rope_store(_dot(u_ref[...], wqka_ref[:, A_WIDTH + p * NW:A_WIDTH + (p + 1) * NW]), 0, ka_ref, p * NW, True)

    ones_row = lax.broadcasted_iota(jnp.int32, (BF16_SUBLANE_PACK, MOBA_BLOCK), 0) == 0
    ones_blk = jnp.where(ones_row, 1.0, 0.0).astype(BF16)
    heads_per_piece = NW // A_HEAD_DIM

    def va_piece(p):
        vat = _dot_nt(wvt_ref[p * NW:(p + 1) * NW, :], u_ref[...])
        for c in range(nchunk):
            cs = slice(c * MOBA_BLOCK, (c + 1) * MOBA_BLOCK)
            for hh in range(heads_per_piece):
                h = p * heads_per_piece + hh
                piece = vat[hh * A_HEAD_DIM:(hh + 1) * A_HEAD_DIM, cs].astype(BF16)
                vat_ref[0, c, h * VA_ROWS:h * VA_ROWS + A_HEAD_DIM, :] = piece
                vat_ref[0, c, h * VA_ROWS + A_HEAD_DIM:(h + 1) * VA_ROWS, :] = ones_blk

    def vb_piece(p):
        vbt = _dot_nt(wvt_ref[A_WIDTH + p * NW:A_WIDTH + (p + 1) * NW, :], u_ref[...])
        for c in range(nchunk):
            vbt_ref[0, c, p * NW:(p + 1) * NW, :] = vbt[:, c * MOBA_BLOCK:(c + 1) * MOBA_BLOCK].astype(BF16)

    @pl.when(t_idx == 0)
    def _():
        zs_ref[:SUBLANES, :] = jnp.zeros((SUBLANES, zs_ref.shape[1]), F32)

    def conv_piece(p):
        ps = slice(p * NW, (p + 1) * NW)
        zs_ref[SUBLANES:, ps] = _dot(u_ref[...], wqkb_ref[:, ps])
        conv = None
        for j in range(CONV_WIDTH):
            off = SUBLANES - (CONV_WIDTH - 1) + j
            term = zs_ref[off:off + tm, ps] * convw_ref[j:j + 1, ps]
            conv = term if conv is None else conv + term
        zs_ref[:SUBLANES, ps] = zs_ref[tm:, ps]
        act = conv * jax.nn.sigmoid(conv)
        if p < B_WIDTH // NW:
            qb_ref[0, :, ps] = act.astype(BF16)
        else:
            kb_ref[0, :, p * NW - B_WIDTH:(p + 1) * NW - B_WIDTH] = (act * (B_HEAD_DIM ** -0.5)).astype(BF16)

    def o_piece(p):
        ps = slice(p * NW, (p + 1) * NW)
        ob_ref[0, :, ps] = _dot(u_ref[...], wo_ref[:, ps]).astype(BF16)

    def gate_piece(p):
        ps = slice(p * NW, (p + 1) * NW)
        sga_ref[0, :, ps] = _dot(u_ref[...], wg_ref[:, ps]).astype(BF16)
        sgb_ref[0, :, ps] = _dot(u_ref[...], wg_ref[:, D_MODEL + p * NW:D_MODEL + (p + 1) * NW]).astype(BF16)

    assert A_WIDTH // NW == 2 and B_WIDTH // NW == 2 and D_MODEL // NW == 4
    q_piece(0); va_piece(0)
    q_piece(1); va_piece(1)
    k_piece(0); vb_piece(0)
    k_piece(1); vb_piece(1)
    conv_piece(0); gate_piece(0)
    conv_piece(1); gate_piece(1)
    conv_piece(2); gate_piece(2); o_piece(0)
    conv_piece(3); gate_piece(3); o_piece(1)

    L = MLSTM_L
    zc = _dot(u, wif_ref[...]) + bcol_ref[...]
    lane8 = lax.broadcasted_iota(jnp.int32, zc.shape, 1)
    pre_c = jnp.where(lane8 < B_HEADS, _log_sigmoid(zc), zc)
    zr = _dot_nt(wift_ref[...], u) + brow_ref[...]
    row8 = lax.broadcasted_iota(jnp.int32, zr.shape, 0)
    pre_r = jnp.where(row8 < B_HEADS, _log_sigmoid(zr), zr)
    ti = lax.broadcasted_iota(jnp.int32, (L, L), 0)
    tj = lax.broadcasted_iota(jnp.int32, (L, L), 1)
    tril = jnp.where(tj <= ti, 1.0, 0.0).astype(BF16)
    triu = jnp.where(ti <= tj, 1.0, 0.0).astype(BF16)
    lane8_c = lax.broadcasted_iota(jnp.int32, (L, 2 * B_HEADS), 1)
    row8_c = lax.broadcasted_iota(jnp.int32, (2 * B_HEADS, L), 0)
    for c in range(tm // L):
        pc = pre_c[c * L:(c + 1) * L, :]
        p1, p2, p3 = _split3(pc)
        cum = _dot(tril, p1) + _dot(tril, p2) + _dot(tril, p3)
        gcol_ref[0, c * L:(c + 1) * L, :] = jnp.where(lane8_c < B_HEADS, cum, pc)
        pr = pre_r[:, c * L:(c + 1) * L]
        r1, r2, r3 = _split3(pr)
        cumr = _dot(r1, triu) + _dot(r2, triu) + _dot(r3, triu)
        grow_ref[0, c] = jnp.where(row8_c < B_HEADS, cumr, pr)


def _in_proj(x, norm_g, rope_t, conv_w, bcol, brow, wqka, wqkb, wo, wg, wif, wvt, wift):
    bn, s, d = x.shape
    tm = TM_IN
    nt = s // tm
    nc = tm // MOBA_BLOCK
    nblk = s // MOBA_BLOCK
    tok = lambda w: pl.BlockSpec((1, tm, w), lambda b, t: (b, t, 0))
    blk4 = lambda r: pl.BlockSpec((1, nc, r, MOBA_BLOCK), lambda b, t: (b, t, 0, 0))
    in_specs = [
        tok(d),
        _const_spec((1, d)),
        pl.BlockSpec((tm, ROPE_TABS * LANES), lambda b, t: (t, 0)),
        _const_spec(conv_w.shape), _const_spec(bcol.shape), _const_spec(brow.shape),
        _const_spec(wqka.shape), _const_spec(wqkb.shape), _const_spec(wo.shape),
        _const_spec(wg.shape), _const_spec(wif.shape), _const_spec(wvt.shape), _const_spec(wift.shape),
    ]
    out_shape = (
        jax.ShapeDtypeStruct((bn, s, A_WIDTH), BF16),
        jax.ShapeDtypeStruct((bn, s, A_WIDTH), BF16),
        jax.ShapeDtypeStruct((bn, nt, nc, A_WIDTH), F32),
        jax.ShapeDtypeStruct((bn, nblk, A_HEADS * VA_ROWS, MOBA_BLOCK), BF16),
        jax.ShapeDtypeStruct((bn, nblk, B_WIDTH, MOBA_BLOCK), BF16),
        jax.ShapeDtypeStruct((bn, s, B_WIDTH), BF16),
        jax.ShapeDtypeStruct((bn, s, B_WIDTH), BF16),
        jax.ShapeDtypeStruct((bn, s, B_WIDTH), BF16),
        jax.ShapeDtypeStruct((bn, s, D_MODEL), BF16),
        jax.ShapeDtypeStruct((bn, s, D_MODEL), BF16),
        jax.ShapeDtypeStruct((bn, nblk, 2 * B_HEADS, MOBA_BLOCK), F32),
        jax.ShapeDtypeStruct((bn, s, 2 * B_HEADS), F32),
    )
    out_specs = (
        tok(A_WIDTH), tok(A_WIDTH),
        pl.BlockSpec((1, 1, nc, A_WIDTH), lambda b, t: (b, t, 0, 0)),
        blk4(A_HEADS * VA_ROWS), blk4(B_WIDTH),
        tok(B_WIDTH), tok(B_WIDTH), tok(B_WIDTH),
        tok(D_MODEL), tok(D_MODEL),
        blk4(2 * B_HEADS),
        tok(2 * B_HEADS),
    )
    return pl.pallas_call(
        _inproj_kernel,
        grid=(bn, nt),
        in_specs=in_specs,
        out_specs=out_specs,
        out_shape=out_shape,
        scratch_shapes=[pltpu.VMEM((tm, d), BF16), pltpu.VMEM((tm + SUBLANES, 2 * B_WIDTH), F32)],
        compiler_params=pltpu.CompilerParams(
            dimension_semantics=("arbitrary", "arbitrary"), vmem_limit_bytes=VMEM_LIMIT_BYTES),
        name="in_proj",
    )(x, norm_g, rope_t, conv_w, bcol, brow, wqka, wqkb, wo, wg, wif, wvt, wift)


def _moba_kernel(q_ref, k_ref, vt_ref, kmean_ref, o_ref, sel_ref, acc_ref, mx_ref, m_ref, s_ref):
    blk = MOBA_BLOCK
    nblk = kmean_ref.shape[1]
    hg = q_ref.shape[2] // A_HEAD_DIM
    i = pl.program_id(2)

    def pair_lanes(h):
        return slice((h // 2) * LANES, (h // 2 + 1) * LANES)

    def head_rows(h):
        return slice(h * VA_ROWS, (h + 1) * VA_ROWS)

    def group_max(st):
        return jnp.max(st.reshape(blk // SUBLANES, SUBLANES, blk), axis=0)

    @pl.when(i == 0)
    def _():
        m_ref[...] = jnp.zeros_like(m_ref)
        o_ref[...] = jnp.zeros_like(o_ref)

    for h in range(hg):
        acc_ref[h] = jnp.zeros(acc_ref.shape[1:], F32)
    m_prev = [m_ref[h, 0:1, :] for h in range(hg)]

    def value_tiles(h, ts):
        part = None
        for t in ts:
            p = jnp.exp2(s_ref[h, t] - m_prev[h]).astype(BF16)
            d = _dot(vt_ref[0, t, head_rows(h), :], p)
            part = d if part is None else part + d
        acc_ref[h] += part

    @pl.when(i < nblk)
    def _():
        lane = lax.broadcasted_iota(jnp.int32, (blk, LANES), 1)
        lo_half = lane < A_HEAD_DIM
        qh = []
        for h in range(hg):
            qp = q_ref[0, :, pair_lanes(h)]
            qh.append(jnp.where(lo_half if h % 2 == 0 else jnp.logical_not(lo_half), qp, jnp.zeros_like(qp)))

        bidx = lax.broadcasted_iota(jnp.int32, (nblk, blk), 0)
        past = bidx < i
        later = [bidx > jp for jp in range(nblk)]
        for h in range(hg):
            km = kmean_ref[0, :, pair_lanes(h)]
            km_hi = km.astype(BF16)
            km_lo = (km - km_hi.astype(F32)).astype(BF16)
            gate = _dot_nt(km_hi, qh[h]) + _dot_nt(km_lo, qh[h])
            gate = jnp.where(past, gate, -jnp.inf)
            beats = []
            for jp in range(nblk):
                gj = gate[jp:jp + 1, :]
                beats.append(jnp.where(later[jp], jnp.where(gj >= gate, 1.0, 0.0), jnp.where(gj > gate, 1.0, 0.0)))
            while len(beats) > 1:
                beats = [a + b for a, b in zip(beats[::2], beats[1::2])]
            sel_ref[h] = jnp.where(past & (beats[0] < MOBA_TOPK), 1.0, 0.0)

        key_i = lax.broadcasted_iota(jnp.int32, (blk, blk), 0)
        qry_i = lax.broadcasted_iota(jnp.int32, (blk, blk), 1)
        causal = key_i <= qry_i

        def score_tile(h, t, m, own):
            kt = k_ref[0, pl.ds(pl.multiple_of(t * blk, blk), blk), pair_lanes(h)]
            keep = causal if own else sel_ref[h, pl.ds(t, 1), :] > 0.5
            st = jnp.where(keep, _dot_nt(kt, qh[h]), -jnp.inf)
            s_ref[h, t] = st
            return jnp.maximum(m, group_max(st))

        def key_blocks(ts, mx):
            out = []
            for h in range(hg):
                value_tiles(h, ts)
                m = mx[h]
                for t in ts:
                    m = score_tile(h, t, m, False)
                out.append(m)
            return tuple(out)

        mx0 = tuple(score_tile(h, i, jnp.full((SUBLANES, blk), -jnp.inf, F32), True) for h in range(hg))
        mx = lax.fori_loop(0, lax.shift_right_logical(i, 1),
                           lambda t2, mx: key_blocks([2 * t2, 2 * t2 + 1], mx), mx0)
        for h in range(hg):
            mx_ref[h] = mx[h]

        @pl.when(lax.rem(i, 2) == 1)
        def _():
            last = key_blocks([i - 1], tuple(mx_ref[h] for h in range(hg)))
            for h in range(hg):
                mx_ref[h] = last[h]

        for h in range(hg):
            m_ref[h] = jnp.broadcast_to(jnp.max(mx_ref[h], axis=0, keepdims=True), m_ref.shape[1:])

    @pl.when(i == nblk)
    def _():
        def body(t2, _):
            for h in range(hg):
                value_tiles(h, [2 * t2, 2 * t2 + 1])
            return 0

        lax.fori_loop(0, nblk // 2, body, 0)

    @pl.when(i > 0)
    def _():
        outs = []
        for h in range(hg):
            acc = acc_ref[h]
            outs.append(acc[:A_HEAD_DIM] / acc[A_HEAD_DIM:A_HEAD_DIM + 1])
        o_ref[0] = jnp.concatenate(outs, axis=0).T.astype(BF16)


def _moba(qa, ka, vat, kmean):
    bn, s, _ = qa.shape
    blk = MOBA_BLOCK
    nblk = s // blk
    hg = MOBA_HEADS_PER_STEP
    ngroup = A_HEADS // hg
    gw = hg * A_HEAD_DIM
    return pl.pallas_call(
        _moba_kernel,
        grid=(bn, ngroup, nblk + 1),
        in_specs=[
            pl.BlockSpec((1, blk, gw), lambda b, g, i: (b, jnp.minimum(i, nblk - 1), g)),
            pl.BlockSpec((1, s, gw), lambda b, g, i: (b, 0, g)),
            pl.BlockSpec((1, nblk, hg * VA_ROWS, blk), lambda b, g, i: (b, 0, g, 0)),
            pl.BlockSpec((1, nblk, gw), lambda b, g, i: (b, 0, g)),
        ],
        out_specs=pl.BlockSpec((1, blk, gw), lambda b, g, i: (b, jnp.maximum(i - 1, 0), g)),
        out_shape=jax.ShapeDtypeStruct((bn, s, A_WIDTH), BF16),
        scratch_shapes=[pltpu.VMEM((hg, nblk, blk), F32),
                        pltpu.VMEM((hg, VA_ROWS, blk), F32),
                        pltpu.VMEM((hg, SUBLANES, blk), F32),
                        pltpu.VMEM((hg, SUBLANES, blk), F32),
                        pltpu.VMEM((hg, nblk, blk, blk), F32)],
        compiler_params=pltpu.CompilerParams(
            dimension_semantics=("arbitrary", "arbitrary", "arbitrary"), vmem_limit_bytes=VMEM_LIMIT_BYTES),
        name="moba_attention",
    )(qa, ka, vat, kmean)


def _mlstm_kernel(q_ref, k_ref, vt_ref, o_ref, grow_ref, gcol_ref, ng_ref, y_ref, ctn_ref, m_ref):
    L = MLSTM_L
    d = B_HEAD_DIM
    nchunk = q_ref.shape[1] // L

    @pl.when(pl.program_id(1) == 0)
    def _():
        ctn_ref[...] = jnp.zeros_like(ctn_ref)
        m_ref[...] = jnp.zeros_like(m_ref)

    s_i = lax.broadcasted_iota(jnp.int32, (L, L), 0)
    t_i = lax.broadcasted_iota(jnp.int32, (L, L), 1)
    causal = s_i <= t_i
    aug_row = lax.broadcasted_iota(jnp.int32, (BF16_SUBLANE_PACK, L), 0)
    ones_blk = jnp.where(aug_row == 0, 1.0, 0.0)

    def chunk(c, _):
        r0 = c * L
        grow = grow_ref[0, c]
        gcol = gcol_ref[0, pl.ds(r0, L), :]
        for h in range(B_HEADS):
            hs = slice(h * d, (h + 1) * d)
            qc = q_ref[0, pl.ds(r0, L), hs]
            kc = k_ref[0, pl.ds(r0, L), hs]
            vt = vt_ref[0, c, hs, :].astype(F32)
            b_row = grow[h:h + 1, :]
            i_row = grow[B_HEADS + h:B_HEADS + h + 1, :]
            r_col = gcol[:, B_HEADS + h:B_HEADS + h + 1] - gcol[:, h:h + 1]
            m_prev = m_ref[h, 0:1, :]

            dm = jnp.where(causal, r_col + b_row, -jnp.inf)
            m_inter = b_row + m_prev
            m_t = jnp.maximum(m_inter, jnp.max(dm, axis=0, keepdims=True))
            pw = (_dot_nt(kc, qc) * jnp.exp(dm - m_t)).astype(BF16)
            w_inter = jnp.exp(m_inter - m_t)
            vt_aug = jnp.concatenate([vt, ones_blk], axis=0).astype(BF16)
            ctn = ctn_ref[h]
            tot = _dot(vt_aug, pw) + w_inter * _dot_nt(ctn.astype(BF16), qc)
            den = tot[d:d + 1, :]
            ht = tot[:d, :] / jnp.maximum(jnp.abs(den), jnp.exp(-m_t))
            ms = jnp.mean(ht * ht, axis=0, keepdims=True)
            hn = (ht * lax.rsqrt(ms + NORM_EPS)).T
            og = jax.nn.sigmoid(o_ref[0, pl.ds(r0, L), hs].astype(F32))
            y_ref[0, pl.ds(r0, L), hs] = (og * (hn * ng_ref[:, hs])).astype(BF16)

            b_tot = b_row[:, L - 1:L]
            g = b_tot - b_row + i_row
            m_new = jnp.maximum(b_tot + m_prev, jnp.max(g, axis=1, keepdims=True))
            w_c = jnp.exp(b_tot + m_prev - m_new)
            w_s = jnp.exp(g - m_new)
            vw = jnp.concatenate([vt * w_s, ones_blk * w_s], axis=0).astype(BF16)
            ctn_ref[h] = w_c[:, :d] * ctn + _dot(vw, kc)
            m_ref[h] = jnp.broadcast_to(m_new, m_ref.shape[1:])
        return 0

    for c in range(nchunk):
        chunk(c, 0)


def _mlstm(qb, kb, vbt, ob, grow, gcol, norm_g):
    bn, s, w = qb.shape
    ts = TS_MLSTM
    nc = ts // MLSTM_L
    tok = pl.BlockSpec((1, ts, w), lambda b, t: (b, t, 0))
    return pl.pallas_call(
        _mlstm_kernel,
        grid=(bn, s // ts),
        in_specs=[
            tok, tok,
            pl.BlockSpec((1, nc, w, MLSTM_L), lambda b, t: (b, t, 0, 0)),
            tok,
            pl.BlockSpec((1, nc, 2 * B_HEADS, MLSTM_L), lambda b, t: (b, t, 0, 0)),
            pl.BlockSpec((1, ts, 2 * B_HEADS), lambda b, t: (b, t, 0)),
            _const_spec((1, w)),
        ],
        out_specs=tok,
        out_shape=jax.ShapeDtypeStruct((bn, s, w), BF16),
        scratch_shapes=[pltpu.VMEM((B_HEADS, AUG_ROWS, B_HEAD_DIM), F32),
                        pltpu.VMEM((B_HEADS, SUBLANES, MLSTM_L), F32)],
        compiler_params=pltpu.CompilerParams(
            dimension_semantics=("arbitrary", "arbitrary"), vmem_limit_bytes=VMEM_LIMIT_BYTES),
        name="mlstm_chunkwise",
    )(qb, kb, vbt, ob, grow, gcol, norm_g)


def _merge_ffn_kernel(x_ref, ya_ref, yb_ref, sga_ref, sgb_ref, wpa_ref, wpb_ref, wout_ref,
                      gffn_ref, wgu_ref, wdown_ref, gfin_ref, o_ref, u_ref, h_ref, act_ref):
    merged = (jax.nn.sigmoid(sga_ref[0].astype(F32)) * _dot(ya_ref[0], wpa_ref[...])
              + jax.nn.sigmoid(sgb_ref[0].astype(F32)) * _dot(yb_ref[0], wpb_ref[...]))
    h = x_ref[0] + _dot(merged.astype(BF16), wout_ref[...])
    var = jnp.mean(h * h, axis=-1, keepdims=True)
    u_ref[...] = (h * lax.rsqrt(var + NORM_EPS) * gffn_ref[...]).astype(BF16)
    h_ref[...] = h

    for c in range(D_FF // FF_CHUNK):
        g = _dot(u_ref[...], wgu_ref[:, c * FF_CHUNK:(c + 1) * FF_CHUNK])
        up = _dot(u_ref[...], wgu_ref[:, D_FF + c * FF_CHUNK:D_FF + (c + 1) * FF_CHUNK])
        act_ref[:, c * FF_CHUNK:(c + 1) * FF_CHUNK] = (g * jax.nn.sigmoid(g) * up).astype(BF16)

    h2 = h_ref[...] + _dot(act_ref[...], wdown_ref[...])
    var2 = jnp.mean(h2 * h2, axis=-1, keepdims=True)
    o_ref[0] = h2 * lax.rsqrt(var2 + NORM_EPS) * gfin_ref[...]


def _merge_ffn(x, ya, yb, sga, sgb, wpa, wpb, wout, gffn, wgu, wdown, gfin):
    bn, s, d = x.shape
    tm = TM_OUT
    tok = lambda w: pl.BlockSpec((1, tm, w), lambda b, t: (b, t, 0))
    return pl.pallas_call(
        _merge_ffn_kernel,
        grid=(bn, s // tm),
        in_specs=[tok(d), tok(A_WIDTH), tok(B_WIDTH), tok(d), tok(d),
                  _const_spec(wpa.shape), _const_spec(wpb.shape), _const_spec(wout.shape),
                  _const_spec(gffn.shape), _const_spec(wgu.shape),
                  _const_spec(wdown.shape), _const_spec(gfin.shape)],
        out_specs=tok(d),
        out_shape=jax.ShapeDtypeStruct((bn, s, d), F32),
        scratch_shapes=[pltpu.VMEM((tm, d), BF16), pltpu.VMEM((tm, d), F32), pltpu.VMEM((tm, D_FF), BF16)],
        compiler_params=pltpu.CompilerParams(
            dimension_semantics=("arbitrary", "arbitrary"), vmem_limit_bytes=VMEM_LIMIT_BYTES),
        name="merge_ffn",
    )(x, ya, yb, sga, sgb, wpa, wpb, wout, gffn, wgu, wdown, gfin)


def kernel(x, norm_mix_g, w_in, conv_w, b_igate, b_fgate, mlstm_norm_g, w_proj_a, w_proj_b,
           w_out, norm_ffn_g, w_gate_up, w_down, norm_final_g):
    bn, s, d = x.shape
    assert d == D_MODEL and s % TS_MLSTM == 0 and s % TM_IN == 0 and s % TM_OUT == 0
    assert norm_mix_g.shape[0] == 1, "single layer"
    w = w_in[0]
    o_qa, o_ka, o_va = 0, A_WIDTH, 2 * A_WIDTH
    o_qkb = 3 * A_WIDTH
    o_vb = o_qkb + 2 * B_WIDTH
    o_ob = o_vb + B_WIDTH
    o_i = o_ob + B_WIDTH
    o_f = o_i + B_HEADS
    o_ga = o_f + B_HEADS
    o_gb = o_ga + D_MODEL

    wqka = w[:, o_qa:o_va].astype(BF16)
    wqkb = w[:, o_qkb:o_vb].astype(BF16)
    wo = w[:, o_ob:o_i].astype(BF16)
    wg = w[:, o_ga:o_gb + D_MODEL].astype(BF16)
    w_fi = jnp.concatenate([w[:, o_f:o_f + B_HEADS], w[:, o_i:o_i + B_HEADS]], axis=1)
    wif = w_fi.astype(BF16)
    wift = w_fi.T.astype(BF16)
    wvt = jnp.concatenate([w[:, o_va:o_qkb], w[:, o_vb:o_ob]], axis=1).T.astype(BF16)
    bias_fi = jnp.concatenate([b_fgate[0], b_igate[0]]).astype(F32)
    bcol = bias_fi[None, :]
    brow = bias_fi[:, None]

    rope_t = _rope_tables(s)
    (qa, ka, kmean, vat, vbt, qb, kb, ob, sga, sgb, grow, gcol) = _in_proj(
        x, norm_mix_g, rope_t, conv_w[0], bcol, brow, wqka, wqkb, wo, wg, wif, wvt, wift)
    kmean = kmean.reshape(bn, s // MOBA_BLOCK, A_WIDTH)

    ya = _moba(qa, ka, vat, kmean)
    yb = _mlstm(qb, kb, vbt, ob, grow, gcol, mlstm_norm_g)

    wgu = w_gate_up[0].astype(BF16)
    return _merge_ffn(x, ya, yb, sga, sgb,
                      w_proj_a[0].astype(BF16), w_proj_b[0].astype(BF16), w_out[0].astype(BF16),
                      norm_ffn_g, wgu, w_down[0].astype(BF16), norm_final_g[None, :])
```

```python
import functools

import jax
import jax.numpy as jnp
import numpy as np
from jax import lax
from jax.experimental import pallas as pl
from jax.experimental.pallas import tpu as pltpu

F32 = jnp.float32
BF16 = jnp.bfloat16

D_MODEL = 1024
A_HEADS = 8
A_HEAD_DIM = 64
A_WIDTH = A_HEADS * A_HEAD_DIM
MOBA_BLOCK = 256
MOBA_TOPK = 3
ROPE_THETA = 10000.0
B_HEADS = 4
B_HEAD_DIM = 128
B_WIDTH = B_HEADS * B_HEAD_DIM
CONV_WIDTH = 4
D_FF = 2816
NORM_EPS = 1e-6

LANES = 128
SUBLANES = 8
BF16_SUBLANE_PACK = 16
VMEM_LIMIT_BYTES = 56 * 1024 * 1024

TM_IN = 512
MLSTM_L = 256
TS_MLSTM = 1024
TM_OUT = 512
FF_CHUNK = 256
IN_CHUNK = 256
AUG_ROWS = B_HEAD_DIM + BF16_SUBLANE_PACK
MOBA_HEADS_PER_STEP = 8
VA_ROWS = A_HEAD_DIM + BF16_SUBLANE_PACK
LOG2E = 1.4426950408889634
Q_SCALE = A_HEAD_DIM ** -0.5 * LOG2E
ROPE_TABS = 6


def _dot(a, b):
    return jnp.dot(a, b, preferred_element_type=F32)


def _dot_nt(a, b):
    return lax.dot_general(a, b, (((1,), (1,)), ((), ())), preferred_element_type=F32)


def _split3(x):
    x1 = x.astype(BF16)
    r1 = x - x1.astype(F32)
    x2 = r1.astype(BF16)
    r2 = r1 - x2.astype(F32)
    return x1, x2, r2.astype(BF16)


def _const_spec(shape):
    nd = len(shape)
    return pl.BlockSpec(shape, lambda *_: (0,) * nd, pipeline_mode=pl.Buffered(1))


def _rope_table_kernel(inv_ref, tab_ref):
    s = tab_ref.shape[0]
    pos = lax.broadcasted_iota(jnp.int32, (s, LANES), 0).astype(F32)
    lane = lax.broadcasted_iota(jnp.int32, (s, LANES), 1)
    ang = pos * inv_ref[...]
    first_half = (lane % A_HEAD_DIM) < (A_HEAD_DIM // 2)
    cs = jnp.cos(ang)
    sn = jnp.sin(ang)
    parts = [cs, jnp.where(first_half, -sn, 0.0), jnp.where(first_half, 0.0, sn)]
    for i, p in enumerate(parts):
        tab_ref[:, i * LANES:(i + 1) * LANES] = p
        tab_ref[:, (i + 3) * LANES:(i + 4) * LANES] = p * Q_SCALE


def _rope_tables(s):
    half = A_HEAD_DIM // 2
    inv = ROPE_THETA ** (-jnp.arange(half, dtype=F32) / half)
    inv = jnp.tile(inv, LANES // half)[None, :]
    return pl.pallas_call(
        _rope_table_kernel,
        out_shape=jax.ShapeDtypeStruct((s, ROPE_TABS * LANES), F32),
        name="rope_tables",
    )(inv)


def _log_sigmoid(x):
    return jnp.minimum(x, 0.0) - jnp.log(1.0 + jnp.exp(-jnp.abs(x)))


def _inproj_kernel(x_ref, g_ref, rope_ref, convw_ref, bcol_ref, brow_ref,
                   wqka_ref, wqkb_ref, wo_ref, wg_ref, wif_ref, wvt_ref, wift_ref,
                   qa_ref, ka_ref, kmean_ref, vat_ref, vbt_ref, qb_ref, kb_ref, ob_ref,
                   sga_ref, sgb_ref, grow_ref, gcol_ref,
                   u_ref, zs_ref):
    tm = x_ref.shape[1]
    nchunk = tm // MOBA_BLOCK
    t_idx = pl.program_id(1)

    xf = x_ref[0]
    var = jnp.mean(xf * xf, axis=-1, keepdims=True)
    u_ref[...] = (xf * lax.rsqrt(var + NORM_EPS) * g_ref[...]).astype(BF16)
    u = u_ref[...]

    half = A_HEAD_DIM // 2

    NW = IN_CHUNK

    def rope_store(z, tab0, out_ref, col0, with_mean):
        c, s_lo, s_hi = [rope_ref[:, (tab0 + i) * LANES:(tab0 + i + 1) * LANES] for i in range(3)]
        for lt in range(NW // LANES):
            zt = z[:, lt * LANES:(lt + 1) * LANES]
            rt = zt * c + pltpu.roll(zt, LANES - half, 1) * s_lo + pltpu.roll(zt, half, 1) * s_hi
            ls = slice(col0 + lt * LANES, col0 + (lt + 1) * LANES)
            out_ref[0, :, ls] = rt.astype(BF16)
            if with_mean:
                for cb in range(nchunk):
                    kmean_ref[0, 0, cb:cb + 1, ls] = jnp.mean(rt[cb * MOBA_BLOCK:(cb + 1) * MOBA_BLOCK],
                                                             axis=0, keepdims=True)

    def q_piece(p):
        rope_store(_dot(u_ref[...], wqka_ref[:, p * NW:(p + 1) * NW]), 3, qa_ref, p * NW, False)

    def k_piece(p):
        rope_store(_dot(u_ref[...], wqka_ref[:, A_WIDTH + p * NW:A_WIDTH + (p + 1) * NW]), 0, ka_ref, p * NW, True)

    ones_row = lax.broadcasted_iota(jnp.int32, (BF16_SUBLANE_PACK, MOBA_BLOCK), 0) == 0
    ones_blk = jnp.where(ones_row, 1.0, 0.0).astype(BF16)
    heads_per_piece = NW // A_HEAD_DIM

    def va_piece(p):
        vat = _dot_nt(wvt_ref[p * NW:(p + 1) * NW, :], u_ref[...])
        for c in range(nchunk):
            cs = slice(c * MOBA_BLOCK, (c + 1) * MOBA_BLOCK)
            for hh in range(heads_per_piece):
                h = p * heads_per_piece + hh
                piece = vat[hh * A_HEAD_DIM:(hh + 1) * A_HEAD_DIM, cs].astype(BF16)
                vat_ref[0, c, h * VA_ROWS:h * VA_ROWS + A_HEAD_DIM, :] = piece
                vat_ref[0, c, h * VA_ROWS + A_HEAD_DIM:(h + 1) * VA_ROWS, :] = ones_blk

    def vb_piece(p):
        vbt = _dot_nt(wvt_ref[A_WIDTH + p * NW:A_WIDTH + (p + 1) * NW, :], u_ref[...])
        for c in range(nchunk):
            vbt_ref[0, c, p * NW:(p + 1) * NW, :] = vbt[:, c * MOBA_BLOCK:(c + 1) * MOBA_BLOCK].astype(BF16)

    @pl.when(t_idx == 0)
    def _():
        zs_ref[:SUBLANES, :] = jnp.zeros((SUBLANES, zs_ref.shape[1]), F32)

    def conv_piece(p):
        ps = slice(p * NW, (p + 1) * NW)
        zs_ref[SUBLANES:, ps] = _dot(u_ref[...], wqkb_ref[:, ps])
        conv = None
        for j in range(CONV_WIDTH):
            off = SUBLANES - (CONV_WIDTH - 1) + j
            term = zs_ref[off:off + tm, ps] * convw_ref[j:j + 1, ps]
            conv = term if conv is None else conv + term
        zs_ref[:SUBLANES, ps] = zs_ref[tm:, ps]
        act = conv * jax.nn.sigmoid(conv)
        if p < B_WIDTH // NW:
            qb_ref[0, :, ps] = act.astype(BF16)
        else:
            kb_ref[0, :, p * NW - B_WIDTH:(p + 1) * NW - B_WIDTH] = (act * (B_HEAD_DIM ** -0.5)).astype(BF16)

    def o_piece(p):
        ps = slice(p * NW, (p + 1) * NW)
        ob_ref[0, :, ps] = _dot(u_ref[...], wo_ref[:, ps]).astype(BF16)

    def gate_piece(p):
        ps = slice(p * NW, (p + 1) * NW)
        sga_ref[0, :, ps] = _dot(u_ref[...], wg_ref[:, ps]).astype(BF16)
        sgb_ref[0, :, ps] = _dot(u_ref[...], wg_ref[:, D_MODEL + p * NW:D_MODEL + (p + 1) * NW]).astype(BF16)

    assert A_WIDTH // NW == 2 and B_WIDTH // NW == 2 and D_MODEL // NW == 4
    q_piece(0); va_piece(0)
    q_piece(1); va_piece(1)
    k_piece(0); vb_piece(0)
    k_piece(1); vb_piece(1)
    conv_piece(0); gate_piece(0)
    conv_piece(1); gate_piece(1)
    conv_piece(2); gate_piece(2); o_piece(0)
    conv_piece(3); gate_piece(3); o_piece(1)

    L = MLSTM_L
    zc = _dot(u, wif_ref[...]) + bcol_ref[...]
    lane8 = lax.broadcasted_iota(jnp.int32, zc.shape, 1)
    pre_c = jnp.where(lane8 < B_HEADS, _log_sigmoid(zc), zc)
    zr = _dot_nt(wift_ref[...], u) + brow_ref[...]
    row8 = lax.broadcasted_iota(jnp.int32, zr.shape, 0)
    pre_r = jnp.where(row8 < B_HEADS, _log_sigmoid(zr), zr)
    ti = lax.broadcasted_iota(jnp.int32, (L, L), 0)
    tj = lax.broadcasted_iota(jnp.int32, (L, L), 1)
    tril = jnp.where(tj <= ti, 1.0, 0.0).astype(BF16)
    triu = jnp.where(ti <= tj, 1.0, 0.0).astype(BF16)
    lane8_c = lax.broadcasted_iota(jnp.int32, (L, 2 * B_HEADS), 1)
    row8_c = lax.broadcasted_iota(jnp.int32, (2 * B_HEADS, L), 0)
    for c in range(tm // L):
        pc = pre_c[c * L:(c + 1) * L, :]
        p1, p2, p3 = _split3(pc)
        cum = _dot(tril, p1) + _dot(tril, p2) + _dot(tril, p3)
        gcol_ref[0, c * L:(c + 1) * L, :] = jnp.where(lane8_c < B_HEADS, cum, pc)
        pr = pre_r[:, c * L:(c + 1) * L]
        r1, r2, r3 = _split3(pr)
        cumr = _dot(r1, triu) + _dot(r2, triu) + _dot(r3, triu)
        grow_ref[0, c] = jnp.where(row8_c < B_HEADS, cumr, pr)


def _in_proj(x, norm_g, rope_t, conv_w, bcol, brow, wqka, wqkb, wo, wg, wif, wvt, wift):
    bn, s, d = x.shape
    tm = TM_IN
    nt = s // tm
    nc = tm // MOBA_BLOCK
    nblk = s // MOBA_BLOCK
    tok = lambda w: pl.BlockSpec((1, tm, w), lambda b, t: (b, t, 0))
    blk4 = lambda r: pl.BlockSpec((1, nc, r, MOBA_BLOCK), lambda b, t: (b, t, 0, 0))
    in_specs = [
        tok(d),
        _const_spec((1, d)),
        pl.BlockSpec((tm, ROPE_TABS * LANES), lambda b, t: (t, 0)),
        _const_spec(conv_w.shape), _const_spec(bcol.shape), _const_spec(brow.shape),
        _const_spec(wqka.shape), _const_spec(wqkb.shape), _const_spec(wo.shape),
        _const_spec(wg.shape), _const_spec(wif.shape), _const_spec(wvt.shape), _const_spec(wift.shape),
    ]
    out_shape = (
        jax.ShapeDtypeStruct((bn, s, A_WIDTH), BF16),
        jax.ShapeDtypeStruct((bn, s, A_WIDTH), BF16),
        jax.ShapeDtypeStruct((bn, nt, nc, A_WIDTH), F32),
        jax.ShapeDtypeStruct((bn, nblk, A_HEADS * VA_ROWS, MOBA_BLOCK), BF16),
        jax.ShapeDtypeStruct((bn, nblk, B_WIDTH, MOBA_BLOCK), BF16),
        jax.ShapeDtypeStruct((bn, s, B_WIDTH), BF16),
        jax.ShapeDtypeStruct((bn, s, B_WIDTH), BF16),
        jax.ShapeDtypeStruct((bn, s, B_WIDTH), BF16),
        jax.ShapeDtypeStruct((bn, s, D_MODEL), BF16),
        jax.ShapeDtypeStruct((bn, s, D_MODEL), BF16),
        jax.ShapeDtypeStruct((bn, nblk, 2 * B_HEADS, MOBA_BLOCK), F32),
        jax.ShapeDtypeStruct((bn, s, 2 * B_HEADS), F32),
    )
    out_specs = (
        tok(A_WIDTH), tok(A_WIDTH),
        pl.BlockSpec((1, 1, nc, A_WIDTH), lambda b, t: (b, t, 0, 0)),
        blk4(A_HEADS * VA_ROWS), blk4(B_WIDTH),
        tok(B_WIDTH), tok(B_WIDTH), tok(B_WIDTH),
        tok(D_MODEL), tok(D_MODEL),
        blk4(2 * B_HEADS),
        tok(2 * B_HEADS),
    )
    return pl.pallas_call(
        _inproj_kernel,
        grid=(bn, nt),
        in_specs=in_specs,
        out_specs=out_specs,
        out_shape=out_shape,
        scratch_shapes=[pltpu.VMEM((tm, d), BF16), pltpu.VMEM((tm + SUBLANES, 2 * B_WIDTH), F32)],
        compiler_params=pltpu.CompilerParams(
            dimension_semantics=("arbitrary", "arbitrary"), vmem_limit_bytes=VMEM_LIMIT_BYTES),
        name="in_proj",
    )(x, norm_g, rope_t, conv_w, bcol, brow, wqka, wqkb, wo, wg, wif, wvt, wift)


def _moba_kernel(q_ref, k_ref, vt_ref, kmean_ref, o_ref, sel_ref, acc_ref, mx_ref, m_ref, s_ref):
    blk = MOBA_BLOCK
    nblk = kmean_ref.shape[1]
    hg = q_ref.shape[2] // A_HEAD_DIM
    i = pl.program_id(2)

    def pair_lanes(h):
        return slice((h // 2) * LANES, (h // 2 + 1) * LANES)

    def head_rows(h):
        return slice(h * VA_ROWS, (h + 1) * VA_ROWS)

    def group_max(st):
        return jnp.max(st.reshape(blk // SUBLANES, SUBLANES, blk), axis=0)

    @pl.when(i == 0)
    def _():
        m_ref[...] = jnp.zeros_like(m_ref)
        o_ref[...] = jnp.zeros_like(o_ref)

    for h in range(hg):
        acc_ref[h] = jnp.zeros(acc_ref.shape[1:], F32)
    m_prev = [m_ref[h, 0:1, :] for h in range(hg)]

    def value_tiles(h, ts):
        part = None
        for t in ts:
            p = jnp.exp2(s_ref[h, t] - m_prev[h]).astype(BF16)
            d = _dot(vt_ref[0, t, head_rows(h), :], p)
            part = d if part is None else part + d
        acc_ref[h] += part

    @pl.when(i < nblk)
    def _():
        lane = lax.broadcasted_iota(jnp.int32, (blk, LANES), 1)
        lo_half = lane < A_HEAD_DIM
        qh = []
        for h in range(hg):
            qp = q_ref[0, :, pair_lanes(h)]
            qh.append(jnp.where(lo_half if h % 2 == 0 else jnp.logical_not(lo_half), qp, jnp.zeros_like(qp)))

        km = kmean_ref[0]
        lane_w = lax.broadcasted_iota(jnp.int32, km.shape, 1)
        kbd = jnp.concatenate(
            [jnp.where((lane_w >= h * A_HEAD_DIM) & (lane_w < (h + 1) * A_HEAD_DIM), km, 0.0) for h in range(hg)],
            axis=0)
        kbd_hi = kbd.astype(BF16)
        kbd_lo = (kbd - kbd_hi.astype(F32)).astype(BF16)
        q_all = q_ref[0]
        gate = (_dot_nt(kbd_hi, q_all) + _dot_nt(kbd_lo, q_all)).reshape(hg, nblk, blk)
        bidx = lax.broadcasted_iota(jnp.int32, (hg, nblk, blk), 1)
        past = bidx < i
        gate = jnp.where(past, gate, -jnp.inf)
        beats = []
        for jp in range(nblk):
            gj = gate[:, jp:jp + 1, :]
            beats.append(jnp.where(bidx > jp, jnp.where(gj >= gate, 1.0, 0.0), jnp.where(gj > gate, 1.0, 0.0)))
        while len(beats) > 1:
            beats = [a + b for a, b in zip(beats[::2], beats[1::2])]
        sel_ref[...] = jnp.where(past & (beats[0] < MOBA_TOPK), 1.0, 0.0)

        key_i = lax.broadcasted_iota(jnp.int32, (blk, blk), 0)
        qry_i = lax.broadcasted_iota(jnp.int32, (blk, blk), 1)
        causal = key_i <= qry_i

        def score_tile(h, t, m, own):
            kt = k_ref[0, pl.ds(pl.multiple_of(t * blk, blk), blk), pair_lanes(h)]
            keep = causal if own else sel_ref[h, pl.ds(t, 1), :] > 0.5
            st = jnp.where(keep, _dot_nt(kt, qh[h]), -jnp.inf)
            s_ref[h, t] = st
            return jnp.maximum(m, group_max(st))

        def key_blocks(ts, mx):
            out = []
            for h in range(hg):
                value_tiles(h, ts)
                m = mx[h]
                for t in ts:
                    m = score_tile(h, t, m, False)
                out.append(m)
            return tuple(out)

        def key_blocks_via_ref(ts):
            out = key_blocks(ts, tuple(mx_ref[h] for h in range(hg)))
            for h in range(hg):
                mx_ref[h] = out[h]

        mx0 = tuple(score_tile(h, i, jnp.full((SUBLANES, blk), -jnp.inf, F32), True) for h in range(hg))
        nquad = lax.shift_right_logical(i, 2)
        mx = lax.fori_loop(
            0, nquad,
            lambda t4, mx: key_blocks([4 * t4 + 2, 4 * t4 + 3], key_blocks([4 * t4, 4 * t4 + 1], mx)), mx0)
        for h in range(hg):
            mx_ref[h] = mx[h]

        @pl.when(lax.rem(lax.shift_right_logical(i, 1), 2) == 1)
        def _():
            key_blocks_via_ref([4 * nquad, 4 * nquad + 1])

        @pl.when(lax.rem(i, 2) == 1)
        def _():
            key_blocks_via_ref([i - 1])

        for h in range(hg):
            m_ref[h] = jnp.broadcast_to(jnp.max(mx_ref[h], axis=0, keepdims=True), m_ref.shape[1:])

    @pl.when(i == nblk)
    def _():
        def body(t2, _):
            for h in range(hg):
                value_tiles(h, [2 * t2, 2 * t2 + 1])
            return 0

        lax.fori_loop(0, nblk // 2, body, 0)

    @pl.when(i > 0)
    def _():
        outs = []
        for h in range(hg):
            acc = acc_ref[h]
            outs.append(acc[:A_HEAD_DIM] / acc[A_HEAD_DIM:A_HEAD_DIM + 1])
        o_ref[0] = jnp.concatenate(outs, axis=0).T.astype(BF16)


def _moba(qa, ka, vat, kmean):
    bn, s, _ = qa.shape
    blk = MOBA_BLOCK
    nblk = s // blk
    hg = MOBA_HEADS_PER_STEP
    ngroup = A_HEADS // hg
    gw = hg * A_HEAD_DIM
    return pl.pallas_call(
        _moba_kernel,
        grid=(bn, ngroup, nblk + 1),
        in_specs=[
            pl.BlockSpec((1, blk, gw), lambda b, g, i: (b, jnp.minimum(i, nblk - 1), g)),
            pl.BlockSpec((1, s, gw), lambda b, g, i: (b, 0, g)),
            pl.BlockSpec((1, nblk, hg * VA_ROWS, blk), lambda b, g, i: (b, 0, g, 0)),
            pl.BlockSpec((1, nblk, gw), lambda b, g, i: (b, 0, g)),
        ],
        out_specs=pl.BlockSpec((1, blk, gw), lambda b, g, i: (b, jnp.maximum(i - 1, 0), g)),
        out_shape=jax.ShapeDtypeStruct((bn, s, A_WIDTH), BF16),
        scratch_shapes=[pltpu.VMEM((hg, nblk, blk), F32),
                        pltpu.VMEM((hg, VA_ROWS, blk), F32),
                        pltpu.VMEM((hg, SUBLANES, blk), F32),
                        pltpu.VMEM((hg, SUBLANES, blk), F32),
                        pltpu.VMEM((hg, nblk, blk, blk), F32)],
        compiler_params=pltpu.CompilerParams(
            dimension_semantics=("arbitrary", "arbitrary", "arbitrary"), vmem_limit_bytes=VMEM_LIMIT_BYTES),
        name="moba_attention",
    )(qa, ka, vat, kmean)


def _mlstm_kernel(q_ref, k_ref, vt_ref, o_ref, grow_ref, gcol_ref, ng_ref, y_ref, ctn_ref, m_ref):
    L = MLSTM_L
    d = B_HEAD_DIM
    nchunk = q_ref.shape[1] // L

    @pl.when(pl.program_id(1) == 0)
    def _():
        ctn_ref[...] = jnp.zeros_like(ctn_ref)
        m_ref[...] = jnp.zeros_like(m_ref)

    s_i = lax.broadcasted_iota(jnp.int32, (L, L), 0)
    t_i = lax.broadcasted_iota(jnp.int32, (L, L), 1)
    causal = s_i <= t_i
    aug_row = lax.broadcasted_iota(jnp.int32, (BF16_SUBLANE_PACK, L), 0)
    ones_blk = jnp.where(aug_row == 0, 1.0, 0.0)

    def chunk(c, _):
        r0 = c * L
        grow = grow_ref[0, c]
        gcol = gcol_ref[0, pl.ds(r0, L), :]
        for h in range(B_HEADS):
            hs = slice(h * d, (h + 1) * d)
            qc = q_ref[0, pl.ds(r0, L), hs]
            kc = k_ref[0, pl.ds(r0, L), hs]
            vt = vt_ref[0, c, hs, :].astype(F32)
            b_row = grow[h:h + 1, :]
            i_row = grow[B_HEADS + h:B_HEADS + h + 1, :]
            r_col = gcol[:, B_HEADS + h:B_HEADS + h + 1] - gcol[:, h:h + 1]
            m_prev = m_ref[h, 0:1, :]

            dm = jnp.where(causal, r_col + b_row, -jnp.inf)
            m_inter = b_row + m_prev
            m_t = jnp.maximum(m_inter, jnp.max(dm, axis=0, keepdims=True))
            pw = (_dot_nt(kc, qc) * jnp.exp(dm - m_t)).astype(BF16)
            w_inter = jnp.exp(m_inter - m_t)
            vt_aug = jnp.concatenate([vt, ones_blk], axis=0).astype(BF16)
            ctn = ctn_ref[h]
            tot = _dot(vt_aug, pw) + w_inter * _dot_nt(ctn.astype(BF16), qc)
            den = tot[d:d + 1, :]
            ht = tot[:d, :] / jnp.maximum(jnp.abs(den), jnp.exp(-m_t))
            ms = jnp.mean(ht * ht, axis=0, keepdims=True)
            hn = (ht * lax.rsqrt(ms + NORM_EPS)).T
            og = jax.nn.sigmoid(o_ref[0, pl.ds(r0, L), hs].astype(F32))
            y_ref[0, pl.ds(r0, L), hs] = (og * (hn * ng_ref[:, hs])).astype(BF16)

            b_tot = b_row[:, L - 1:L]
            g = b_tot - b_row + i_row
            m_new = jnp.maximum(b_tot + m_prev, jnp.max(g, axis=1, keepdims=True))
            w_c = jnp.exp(b_tot + m_prev - m_new)
            w_s = jnp.exp(g - m_new)
            vw = jnp.concatenate([vt * w_s, ones_blk * w_s], axis=0).astype(BF16)
            ctn_ref[h] = w_c[:, :d] * ctn + _dot(vw, kc)
            m_ref[h] = jnp.broadcast_to(m_new, m_ref.shape[1:])
        return 0

    for c in range(nchunk):
        chunk(c, 0)


def _mlstm(qb, kb, vbt, ob, grow, gcol, norm_g):
    bn, s, w = qb.shape
    ts = TS_MLSTM
    nc = ts // MLSTM_L
    tok = pl.BlockSpec((1, ts, w), lambda b, t: (b, t, 0))
    return pl.pallas_call(
        _mlstm_kernel,
        grid=(bn, s // ts),
        in_specs=[
            tok, tok,
            pl.BlockSpec((1, nc, w, MLSTM_L), lambda b, t: (b, t, 0, 0)),
            tok,
            pl.BlockSpec((1, nc, 2 * B_HEADS, MLSTM_L), lambda b, t: (b, t, 0, 0)),
            pl.BlockSpec((1, ts, 2 * B_HEADS), lambda b, t: (b, t, 0)),
            _const_spec((1, w)),
        ],
        out_specs=tok,
        out_shape=jax.ShapeDtypeStruct((bn, s, w), BF16),
        scratch_shapes=[pltpu.VMEM((B_HEADS, AUG_ROWS, B_HEAD_DIM), F32),
                        pltpu.VMEM((B_HEADS, SUBLANES, MLSTM_L), F32)],
        compiler_params=pltpu.CompilerParams(
            dimension_semantics=("arbitrary", "arbitrary"), vmem_limit_bytes=VMEM_LIMIT_BYTES),
        name="mlstm_chunkwise",
    )(qb, kb, vbt, ob, grow, gcol, norm_g)


def _merge_ffn_kernel(x_ref, ya_ref, yb_ref, sga_ref, sgb_ref, wpa_ref, wpb_ref, wout_ref,
                      gffn_ref, wgu_ref, wdown_ref, gfin_ref, o_ref, u_ref, h_ref, act_ref):
    merged = (jax.nn.sigmoid(sga_ref[0].astype(F32)) * _dot(ya_ref[0], wpa_ref[...])
              + jax.nn.sigmoid(sgb_ref[0].astype(F32)) * _dot(yb_ref[0], wpb_ref[...]))
    h = x_ref[0] + _dot(merged.astype(BF16), wout_ref[...])
    var = jnp.mean(h * h, axis=-1, keepdims=True)
    u_ref[...] = (h * lax.rsqrt(var + NORM_EPS) * gffn_ref[...]).astype(BF16)
    h_ref[...] = h

    for c in range(D_FF // FF_CHUNK):
        g = _dot(u_ref[...], wgu_ref[:, c * FF_CHUNK:(c + 1) * FF_CHUNK])
        up = _dot(u_ref[...], wgu_ref[:, D_FF + c * FF_CHUNK:D_FF + (c + 1) * FF_CHUNK])
        act_ref[:, c * FF_CHUNK:(c + 1) * FF_CHUNK] = (g * jax.nn.sigmoid(g) * up).astype(BF16)

    h2 = h_ref[...] + _dot(act_ref[...], wdown_ref[...])
    var2 = jnp.mean(h2 * h2, axis=-1, keepdims=True)
    o_ref[0] = h2 * lax.rsqrt(var2 + NORM_EPS) * gfin_ref[...]


def _merge_ffn(x, ya, yb, sga, sgb, wpa, wpb, wout, gffn, wgu, wdown, gfin):
    bn, s, d = x.shape
    tm = TM_OUT
    tok = lambda w: pl.BlockSpec((1, tm, w), lambda b, t: (b, t, 0))
    return pl.pallas_call(
        _merge_ffn_kernel,
        grid=(bn, s // tm),
        in_specs=[tok(d), tok(A_WIDTH), tok(B_WIDTH), tok(d), tok(d),
                  _const_spec(wpa.shape), _const_spec(wpb.shape), _const_spec(wout.shape),
                  _const_spec(gffn.shape), _const_spec(wgu.shape),
                  _const_spec(wdown.shape), _const_spec(gfin.shape)],
        out_specs=tok(d),
        out_shape=jax.ShapeDtypeStruct((bn, s, d), F32),
        scratch_shapes=[pltpu.VMEM((tm, d), BF16), pltpu.VMEM((tm, d), F32), pltpu.VMEM((tm, D_FF), BF16)],
        compiler_params=pltpu.CompilerParams(
            dimension_semantics=("arbitrary", "arbitrary"), vmem_limit_bytes=VMEM_LIMIT_BYTES),
        name="merge_ffn",
    )(x, ya, yb, sga, sgb, wpa, wpb, wout, gffn, wgu, wdown, gfin)


def kernel(x, norm_mix_g, w_in, conv_w, b_igate, b_fgate, mlstm_norm_g, w_proj_a, w_proj_b,
           w_out, norm_ffn_g, w_gate_up, w_down, norm_final_g):
    bn, s, d = x.shape
    assert d == D_MODEL and s % TS_MLSTM == 0 and s % TM_IN == 0 and s % TM_OUT == 0
    assert norm_mix_g.shape[0] == 1, "single layer"
    w = w_in[0]
    o_qa, o_ka, o_va = 0, A_WIDTH, 2 * A_WIDTH
    o_qkb = 3 * A_WIDTH
    o_vb = o_qkb + 2 * B_WIDTH
    o_ob = o_vb + B_WIDTH
    o_i = o_ob + B_WIDTH
    o_f = o_i + B_HEADS
    o_ga = o_f + B_HEADS
    o_gb = o_ga + D_MODEL

    wqka = w[:, o_qa:o_va].astype(BF16)
    wqkb = w[:, o_qkb:o_vb].astype(BF16)
    wo = w[:, o_ob:o_i].astype(BF16)
    wg = w[:, o_ga:o_gb + D_MODEL].astype(BF16)
    w_fi = jnp.concatenate([w[:, o_f:o_f + B_HEADS], w[:, o_i:o_i + B_HEADS]], axis=1)
    wif = w_fi.astype(BF16)
    wift = w_fi.T.astype(BF16)
    wvt = jnp.concatenate([w[:, o_va:o_qkb], w[:, o_vb:o_ob]], axis=1).T.astype(BF16)
    bias_fi = jnp.concatenate([b_fgate[0], b_igate[0]]).astype(F32)
    bcol = bias_fi[None, :]
    brow = bias_fi[:, None]

    rope_t = _rope_tables(s)
    (qa, ka, kmean, vat, vbt, qb, kb, ob, sga, sgb, grow, gcol) = _in_proj(
        x, norm_mix_g, rope_t, conv_w[0], bcol, brow, wqka, wqkb, wo, wg, wif, wvt, wift)
    kmean = kmean.reshape(bn, s // MOBA_BLOCK, A_WIDTH)

    ya = _moba(qa, ka, vat, kmean)
    yb = _mlstm(qb, kb, vbt, ob, grow, gcol, mlstm_norm_g)

    wgu = w_gate_up[0].astype(BF16)
    return _merge_ffn(x, ya, yb, sga, sgb,
                      w_proj_a[0].astype(BF16), w_proj_b[0].astype(BF16), w_out[0].astype(BF16),
                      norm_ffn_g, wgu, w_down[0].astype(BF16), norm_final_g[None, :])
```

```python
import functools

import jax
import jax.numpy as jnp
import numpy as np
from jax import lax
from jax.experimental import pallas as pl
from jax.experimental.pallas import tpu as pltpu

F32 = jnp.float32
BF16 = jnp.bfloat16

D_MODEL = 1024
A_HEADS = 8
A_HEAD_DIM = 64
A_WIDTH = A_HEADS * A_HEAD_DIM
MOBA_BLOCK = 256
MOBA_TOPK = 3
ROPE_THETA = 10000.0
B_HEADS = 4
B_HEAD_DIM = 128
B_WIDTH = B_HEADS * B_HEAD_DIM
CONV_WIDTH = 4
D_FF = 2816
NORM_EPS = 1e-6

LANES = 128
SUBLANES = 8
BF16_SUBLANE_PACK = 16
VMEM_LIMIT_BYTES = 56 * 1024 * 1024

TM_IN = 512
MLSTM_L = 256
TS_MLSTM = 1024
TM_OUT = 512
FF_CHUNK = 256
IN_CHUNK = 256
FINISH_AFTER_CHUNK = 3
AUG_ROWS = B_HEAD_DIM + BF16_SUBLANE_PACK
MOBA_HEADS_PER_STEP = 8
VA_ROWS = A_HEAD_DIM + BF16_SUBLANE_PACK
LOG2E = 1.4426950408889634
Q_SCALE = A_HEAD_DIM ** -0.5 * LOG2E
ROPE_TABS = 6
ROPE_ROWS = 64


def _dot(a, b):
    return jnp.dot(a, b, preferred_element_type=F32)


def _dot_nt(a, b):
    return lax.dot_general(a, b, (((1,), (1,)), ((), ())), preferred_element_type=F32)


def _split3(x):
    x1 = x.astype(BF16)
    r1 = x - x1.astype(F32)
    x2 = r1.astype(BF16)
    r2 = r1 - x2.astype(F32)
    return x1, x2, r2.astype(BF16)


def _const_spec(shape):
    nd = len(shape)
    return pl.BlockSpec(shape, lambda *_: (0,) * nd, pipeline_mode=pl.Buffered(1))


def _rope_table_kernel(inv_ref, tab_ref):
    s = tab_ref.shape[0]
    r = ROPE_ROWS
    row = lax.broadcasted_iota(jnp.int32, (r, LANES), 0).astype(F32)
    lane = lax.broadcasted_iota(jnp.int32, (r, LANES), 1)
    first_half = (lane % A_HEAD_DIM) < (A_HEAD_DIM // 2)
    ang_b = row * inv_ref[...]
    cb, sb = jnp.cos(ang_b), jnp.sin(ang_b)
    ang_a = (row * float(r)) * inv_ref[...]
    ca, sa = jnp.cos(ang_a), jnp.sin(ang_a)
    for a in range(s // r):
        cs = ca[a:a + 1, :] * cb - sa[a:a + 1, :] * sb
        sn = sa[a:a + 1, :] * cb + ca[a:a + 1, :] * sb
        parts = [cs, jnp.where(first_half, -sn, 0.0), jnp.where(first_half, 0.0, sn)]
        for i, p in enumerate(parts):
            tab_ref[a * r:(a + 1) * r, i * LANES:(i + 1) * LANES] = p
            tab_ref[a * r:(a + 1) * r, (i + 3) * LANES:(i + 4) * LANES] = p * Q_SCALE


def _rope_tables(s):
    half = A_HEAD_DIM // 2
    inv = ROPE_THETA ** (-jnp.arange(half, dtype=F32) / half)
    inv = jnp.tile(inv, LANES // half)[None, :]
    return pl.pallas_call(
        _rope_table_kernel,
        out_shape=jax.ShapeDtypeStruct((s, ROPE_TABS * LANES), F32),
        name="rope_tables",
    )(inv)


def _log_sigmoid(x):
    return jnp.minimum(x, 0.0) - jnp.log(1.0 + jnp.exp(-jnp.abs(x)))


def _inproj_kernel(x_ref, g_ref, rope_ref, convw_ref, bcol_ref, brow_ref,
                   wqka_ref, wqkb_ref, wo_ref, wg_ref, wif_ref, wvt_ref, wift_ref,
                   qa_ref, ka_ref, kmean_ref, vat_ref, vbt_ref, qb_ref, kb_ref, ob_ref,
                   sga_ref, sgb_ref, grow_ref, gcol_ref,
                   u_ref, zs_ref):
    tm = x_ref.shape[1]
    nchunk = tm // MOBA_BLOCK
    t_idx = pl.program_id(1)

    xf = x_ref[0]
    var = jnp.mean(xf * xf, axis=-1, keepdims=True)
    u_ref[...] = (xf * lax.rsqrt(var + NORM_EPS) * g_ref[...]).astype(BF16)
    u = u_ref[...]

    half = A_HEAD_DIM // 2

    NW = IN_CHUNK

    def rope_store(z, tab0, out_ref, col0, with_mean):
        c, s_lo, s_hi = [rope_ref[:, (tab0 + i) * LANES:(tab0 + i + 1) * LANES] for i in range(3)]
        for lt in range(NW // LANES):
            zt = z[:, lt * LANES:(lt + 1) * LANES]
            rt = zt * c + pltpu.roll(zt, LANES - half, 1) * s_lo + pltpu.roll(zt, half, 1) * s_hi
            ls = slice(col0 + lt * LANES, col0 + (lt + 1) * LANES)
            out_ref[0, :, ls] = rt.astype(BF16)
            if with_mean:
                for cb in range(nchunk):
                    kmean_ref[0, 0, cb:cb + 1, ls] = jnp.mean(rt[cb * MOBA_BLOCK:(cb + 1) * MOBA_BLOCK],
                                                             axis=0, keepdims=True)

    def q_piece(p):
        rope_store(_dot(u_ref[...], wqka_ref[:, p * NW:(p + 1) * NW]), 3, qa_ref, p * NW, False)

    def k_piece(p):
        rope_store(_dot(u_ref[...], wqka_ref[:, A_WIDTH + p * NW:A_WIDTH + (p + 1) * NW]), 0, ka_ref, p * NW, True)

    ones_row = lax.broadcasted_iota(jnp.int32, (BF16_SUBLANE_PACK, MOBA_BLOCK), 0) == 0
    ones_blk = jnp.where(ones_row, 1.0, 0.0).astype(BF16)
    heads_per_piece = NW // A_HEAD_DIM

    def va_piece(p):
        vat = _dot_nt(wvt_ref[p * NW:(p + 1) * NW, :], u_ref[...])
        for c in range(nchunk):
            cs = slice(c * MOBA_BLOCK, (c + 1) * MOBA_BLOCK)
            for hh in range(heads_per_piece):
                h = p * heads_per_piece + hh
                piece = vat[hh * A_HEAD_DIM:(hh + 1) * A_HEAD_DIM, cs].astype(BF16)
                vat_ref[0, c, h * VA_ROWS:h * VA_ROWS + A_HEAD_DIM, :] = piece
                vat_ref[0, c, h * VA_ROWS + A_HEAD_DIM:(h + 1) * VA_ROWS, :] = ones_blk

    def vb_piece(p):
        vbt = _dot_nt(wvt_ref[A_WIDTH + p * NW:A_WIDTH + (p + 1) * NW, :], u_ref[...])
        for c in range(nchunk):
            vbt_ref[0, c, p * NW:(p + 1) * NW, :] = vbt[:, c * MOBA_BLOCK:(c + 1) * MOBA_BLOCK].astype(BF16)

    @pl.when(t_idx == 0)
    def _():
        zs_ref[:SUBLANES, :] = jnp.zeros((SUBLANES, zs_ref.shape[1]), F32)

    def conv_piece(p):
        ps = slice(p * NW, (p + 1) * NW)
        zs_ref[SUBLANES:, ps] = _dot(u_ref[...], wqkb_ref[:, ps])
        conv = None
        for j in range(CONV_WIDTH):
            off = SUBLANES - (CONV_WIDTH - 1) + j
            term = zs_ref[off:off + tm, ps] * convw_ref[j:j + 1, ps]
            conv = term if conv is None else conv + term
        zs_ref[:SUBLANES, ps] = zs_ref[tm:, ps]
        act = conv * jax.nn.sigmoid(conv)
        if p < B_WIDTH // NW:
            qb_ref[0, :, ps] = act.astype(BF16)
        else:
            kb_ref[0, :, p * NW - B_WIDTH:(p + 1) * NW - B_WIDTH] = (act * (B_HEAD_DIM ** -0.5)).astype(BF16)

    def o_piece(p):
        ps = slice(p * NW, (p + 1) * NW)
        ob_ref[0, :, ps] = _dot(u_ref[...], wo_ref[:, ps]).astype(BF16)

    def gate_piece(p):
        ps = slice(p * NW, (p + 1) * NW)
        sga_ref[0, :, ps] = _dot(u_ref[...], wg_ref[:, ps]).astype(BF16)
        sgb_ref[0, :, ps] = _dot(u_ref[...], wg_ref[:, D_MODEL + p * NW:D_MODEL + (p + 1) * NW]).astype(BF16)

    assert A_WIDTH // NW == 2 and B_WIDTH // NW == 2 and D_MODEL // NW == 4
    q_piece(0); va_piece(0)
    q_piece(1); va_piece(1)
    k_piece(0); vb_piece(0)
    k_piece(1); vb_piece(1)
    conv_piece(0); gate_piece(0)
    conv_piece(1); gate_piece(1)
    conv_piece(2); gate_piece(2); o_piece(0)
    conv_piece(3); gate_piece(3); o_piece(1)

    L = MLSTM_L
    zc = _dot(u, wif_ref[...]) + bcol_ref[...]
    lane8 = lax.broadcasted_iota(jnp.int32, zc.shape, 1)
    pre_c = jnp.where(lane8 < B_HEADS, _log_sigmoid(zc), zc)
    zr = _dot_nt(wift_ref[...], u) + brow_ref[...]
    row8 = lax.broadcasted_iota(jnp.int32, zr.shape, 0)
    pre_r = jnp.where(row8 < B_HEADS, _log_sigmoid(zr), zr)
    ti = lax.broadcasted_iota(jnp.int32, (L, L), 0)
    tj = lax.broadcasted_iota(jnp.int32, (L, L), 1)
    tril = jnp.where(tj <= ti, 1.0, 0.0).astype(BF16)
    triu = jnp.where(ti <= tj, 1.0, 0.0).astype(BF16)
    lane8_c = lax.broadcasted_iota(jnp.int32, (L, 2 * B_HEADS), 1)
    row8_c = lax.broadcasted_iota(jnp.int32, (2 * B_HEADS, L), 0)
    for c in range(tm // L):
        pc = pre_c[c * L:(c + 1) * L, :]
        p1, p2, p3 = _split3(pc)
        cum = _dot(tril, p1) + _dot(tril, p2) + _dot(tril, p3)
        gcol_ref[0, c * L:(c + 1) * L, :] = jnp.where(lane8_c < B_HEADS, cum, pc)
        pr = pre_r[:, c * L:(c + 1) * L]
        r1, r2, r3 = _split3(pr)
        cumr = _dot(r1, triu) + _dot(r2, triu) + _dot(r3, triu)
        grow_ref[0, c] = jnp.where(row8_c < B_HEADS, cumr, pr)


def _in_proj(x, norm_g, rope_t, conv_w, bcol, brow, wqka, wqkb, wo, wg, wif, wvt, wift):
    bn, s, d = x.shape
    tm = TM_IN
    nt = s // tm
    nc = tm // MOBA_BLOCK
    nblk = s // MOBA_BLOCK
    tok = lambda w: pl.BlockSpec((1, tm, w), lambda b, t: (b, t, 0))
    blk4 = lambda r: pl.BlockSpec((1, nc, r, MOBA_BLOCK), lambda b, t: (b, t, 0, 0))
    in_specs = [
        tok(d),
        _const_spec((1, d)),
        pl.BlockSpec((tm, ROPE_TABS * LANES), lambda b, t: (t, 0)),
        _const_spec(conv_w.shape), _const_spec(bcol.shape), _const_spec(brow.shape),
        _const_spec(wqka.shape), _const_spec(wqkb.shape), _const_spec(wo.shape),
        _const_spec(wg.shape), _const_spec(wif.shape), _const_spec(wvt.shape), _const_spec(wift.shape),
    ]
    out_shape = (
        jax.ShapeDtypeStruct((bn, s, A_WIDTH), BF16),
        jax.ShapeDtypeStruct((bn, s, A_WIDTH), BF16),
        jax.ShapeDtypeStruct((bn, nt, nc, A_WIDTH), F32),
        jax.ShapeDtypeStruct((bn, nblk, A_HEADS * VA_ROWS, MOBA_BLOCK), BF16),
        jax.ShapeDtypeStruct((bn, nblk, B_WIDTH, MOBA_BLOCK), BF16),
        jax.ShapeDtypeStruct((bn, s, B_WIDTH), BF16),
        jax.ShapeDtypeStruct((bn, s, B_WIDTH), BF16),
        jax.ShapeDtypeStruct((bn, s, B_WIDTH), BF16),
        jax.ShapeDtypeStruct((bn, s, D_MODEL), BF16),
        jax.ShapeDtypeStruct((bn, s, D_MODEL), BF16),
        jax.ShapeDtypeStruct((bn, nblk, 2 * B_HEADS, MOBA_BLOCK), F32),
        jax.ShapeDtypeStruct((bn, s, 2 * B_HEADS), F32),
    )
    out_specs = (
        tok(A_WIDTH), tok(A_WIDTH),
        pl.BlockSpec((1, 1, nc, A_WIDTH), lambda b, t: (b, t, 0, 0)),
        blk4(A_HEADS * VA_ROWS), blk4(B_WIDTH),
        tok(B_WIDTH), tok(B_WIDTH), tok(B_WIDTH),
        tok(D_MODEL), tok(D_MODEL),
        blk4(2 * B_HEADS),
        tok(2 * B_HEADS),
    )
    return pl.pallas_call(
        _inproj_kernel,
        grid=(bn, nt),
        in_specs=in_specs,
        out_specs=out_specs,
        out_shape=out_shape,
        scratch_shapes=[pltpu.VMEM((tm, d), BF16), pltpu.VMEM((tm + SUBLANES, 2 * B_WIDTH), F32)],
        compiler_params=pltpu.CompilerParams(
            dimension_semantics=("arbitrary", "arbitrary"), vmem_limit_bytes=VMEM_LIMIT_BYTES),
        name="in_proj",
    )(x, norm_g, rope_t, conv_w, bcol, brow, wqka, wqkb, wo, wg, wif, wvt, wift)


def _moba_kernel(q_ref, k_ref, vt_ref, kmean_ref, o_ref, sel_ref, acc_ref, mx_ref, m_ref, s_ref):
    blk = MOBA_BLOCK
    nblk = kmean_ref.shape[1]
    hg = q_ref.shape[2] // A_HEAD_DIM
    i = pl.program_id(2)

    def pair_lanes(h):
        return slice((h // 2) * LANES, (h // 2 + 1) * LANES)

    def head_rows(h):
        return slice(h * VA_ROWS, (h + 1) * VA_ROWS)

    def group_max(st):
        return jnp.max(st.reshape(blk // SUBLANES, SUBLANES, blk), axis=0)

    @pl.when(i == 0)
    def _():
        m_ref[...] = jnp.zeros_like(m_ref)
        o_ref[...] = jnp.zeros_like(o_ref)

    for h in range(hg):
        acc_ref[h] = jnp.zeros(acc_ref.shape[1:], F32)
    m_prev = [m_ref[h, 0:1, :] for h in range(hg)]

    def value_tiles(h, ts):
        part = None
        for t in ts:
            p = jnp.exp2(s_ref[h, t] - m_prev[h]).astype(BF16)
            d = _dot(vt_ref[0, t, head_rows(h), :], p)
            part = d if part is None else part + d
        acc_ref[h] += part

    @pl.when(i < nblk)
    def _():
        lane = lax.broadcasted_iota(jnp.int32, (blk, LANES), 1)
        lo_half = lane < A_HEAD_DIM
        qh = []
        for h in range(hg):
            qp = q_ref[0, :, pair_lanes(h)]
            qh.append(jnp.where(lo_half if h % 2 == 0 else jnp.logical_not(lo_half), qp, jnp.zeros_like(qp)))

        km = kmean_ref[0]
        lane_w = lax.broadcasted_iota(jnp.int32, km.shape, 1)
        kbd = jnp.concatenate(
            [jnp.where((lane_w >= h * A_HEAD_DIM) & (lane_w < (h + 1) * A_HEAD_DIM), km, 0.0) for h in range(hg)],
            axis=0)
        kbd_hi = kbd.astype(BF16)
        kbd_lo = (kbd - kbd_hi.astype(F32)).astype(BF16)
        q_all = q_ref[0]
        gate = (_dot_nt(kbd_hi, q_all) + _dot_nt(kbd_lo, q_all)).reshape(hg, nblk, blk)
        bidx = lax.broadcasted_iota(jnp.int32, (hg, nblk, blk), 1)
        past = bidx < i
        gate = jnp.where(past, gate, -jnp.inf)
        beats = []
        for jp in range(nblk):
            gj = gate[:, jp:jp + 1, :]
            beats.append(jnp.where(bidx > jp, jnp.where(gj >= gate, 1.0, 0.0), jnp.where(gj > gate, 1.0, 0.0)))
        while len(beats) > 1:
            beats = [a + b for a, b in zip(beats[::2], beats[1::2])]
        sel_ref[...] = jnp.where(past & (beats[0] < MOBA_TOPK), 1.0, 0.0)

        key_i = lax.broadcasted_iota(jnp.int32, (blk, blk), 0)
        qry_i = lax.broadcasted_iota(jnp.int32, (blk, blk), 1)
        causal = key_i <= qry_i

        def score_tile(h, t, m, own):
            kt = k_ref[0, pl.ds(pl.multiple_of(t * blk, blk), blk), pair_lanes(h)]
            keep = causal if own else sel_ref[h, pl.ds(t, 1), :] > 0.5
            st = jnp.where(keep, _dot_nt(kt, qh[h]), -jnp.inf)
            s_ref[h, t] = st
            return jnp.maximum(m, group_max(st))

        def key_blocks(ts, mx):
            out = []
            for h in range(hg):
                value_tiles(h, ts)
                m = mx[h]
                for t in ts:
                    m = score_tile(h, t, m, False)
                out.append(m)
            return tuple(out)

        def key_blocks_via_ref(ts):
            out = key_blocks(ts, tuple(mx_ref[h] for h in range(hg)))
            for h in range(hg):
                mx_ref[h] = out[h]

        mx0 = tuple(score_tile(h, i, jnp.full((SUBLANES, blk), -jnp.inf, F32), True) for h in range(hg))
        nquad = lax.shift_right_logical(i, 2)
        mx = lax.fori_loop(
            0, nquad,
            lambda t4, mx: key_blocks([4 * t4 + 2, 4 * t4 + 3], key_blocks([4 * t4, 4 * t4 + 1], mx)), mx0)
        for h in range(hg):
            mx_ref[h] = mx[h]

        @pl.when(lax.rem(lax.shift_right_logical(i, 1), 2) == 1)
        def _():
            key_blocks_via_ref([4 * nquad, 4 * nquad + 1])

        @pl.when(lax.rem(i, 2) == 1)
        def _():
            key_blocks_via_ref([i - 1])

        for h in range(hg):
            m_ref[h] = jnp.broadcast_to(jnp.max(mx_ref[h], axis=0, keepdims=True), m_ref.shape[1:])

    @pl.when(i == nblk)
    def _():
        def body(t2, _):
            for h in range(hg):
                value_tiles(h, [2 * t2, 2 * t2 + 1])
            return 0

        lax.fori_loop(0, nblk // 2, body, 0)

    @pl.when(i > 0)
    def _():
        outs = []
        for h in range(hg):
            acc = acc_ref[h]
            outs.append(acc[:A_HEAD_DIM] / acc[A_HEAD_DIM:A_HEAD_DIM + 1])
        o_ref[0] = jnp.concatenate(outs, axis=0).T.astype(BF16)


def _moba(qa, ka, vat, kmean):
    bn, s, _ = qa.shape
    blk = MOBA_BLOCK
    nblk = s // blk
    hg = MOBA_HEADS_PER_STEP
    ngroup = A_HEADS // hg
    gw = hg * A_HEAD_DIM
    return pl.pallas_call(
        _moba_kernel,
        grid=(bn, ngroup, nblk + 1),
        in_specs=[
            pl.BlockSpec((1, blk, gw), lambda b, g, i: (b, jnp.minimum(i, nblk - 1), g)),
            pl.BlockSpec((1, s, gw), lambda b, g, i: (b, 0, g)),
            pl.BlockSpec((1, nblk, hg * VA_ROWS, blk), lambda b, g, i: (b, 0, g, 0)),
            pl.BlockSpec((1, nblk, gw), lambda b, g, i: (b, 0, g)),
        ],
        out_specs=pl.BlockSpec((1, blk, gw), lambda b, g, i: (b, jnp.maximum(i - 1, 0), g)),
        out_shape=jax.ShapeDtypeStruct((bn, s, A_WIDTH), BF16),
        scratch_shapes=[pltpu.VMEM((hg, nblk, blk), F32),
                        pltpu.VMEM((hg, VA_ROWS, blk), F32),
                        pltpu.VMEM((hg, SUBLANES, blk), F32),
                        pltpu.VMEM((hg, SUBLANES, blk), F32),
                        pltpu.VMEM((hg, nblk, blk, blk), F32)],
        compiler_params=pltpu.CompilerParams(
            dimension_semantics=("arbitrary", "arbitrary", "arbitrary"), vmem_limit_bytes=VMEM_LIMIT_BYTES),
        name="moba_attention",
    )(qa, ka, vat, kmean)


def _mlstm_kernel(q_ref, k_ref, vt_ref, o_ref, grow_ref, gcol_ref, ng_ref, y_ref, ctn_ref, m_ref):
    L = MLSTM_L
    d = B_HEAD_DIM
    nchunk = q_ref.shape[1] // L

    @pl.when(pl.program_id(1) == 0)
    def _():
        ctn_ref[...] = jnp.zeros_like(ctn_ref)
        m_ref[...] = jnp.zeros_like(m_ref)

    s_i = lax.broadcasted_iota(jnp.int32, (L, L), 0)
    t_i = lax.broadcasted_iota(jnp.int32, (L, L), 1)
    causal = s_i <= t_i
    aug_row = lax.broadcasted_iota(jnp.int32, (BF16_SUBLANE_PACK, L), 0)
    ones_blk = jnp.where(aug_row == 0, 1.0, 0.0)

    def chunk(c, _):
        r0 = c * L
        grow = grow_ref[0, c]
        gcol = gcol_ref[0, pl.ds(r0, L), :]
        for h in range(B_HEADS):
            hs = slice(h * d, (h + 1) * d)
            qc = q_ref[0, pl.ds(r0, L), hs]
            kc = k_ref[0, pl.ds(r0, L), hs]
            vt = vt_ref[0, c, hs, :].astype(F32)
            b_row = grow[h:h + 1, :]
            i_row = grow[B_HEADS + h:B_HEADS + h + 1, :]
            r_col = gcol[:, B_HEADS + h:B_HEADS + h + 1] - gcol[:, h:h + 1]
            m_prev = m_ref[h, 0:1, :]

            dm = jnp.where(causal, r_col + b_row, -jnp.inf)
            m_inter = b_row + m_prev
            m_t = jnp.maximum(m_inter, jnp.max(dm, axis=0, keepdims=True))
            pw = (_dot_nt(kc, qc) * jnp.exp(dm - m_t)).astype(BF16)
            w_inter = jnp.exp(m_inter - m_t)
            vt_aug = jnp.concatenate([vt, ones_blk], axis=0).astype(BF16)
            ctn = ctn_ref[h]
            tot = _dot(vt_aug, pw) + w_inter * _dot_nt(ctn.astype(BF16), qc)
            den = tot[d:d + 1, :]
            ht = tot[:d, :] / jnp.maximum(jnp.abs(den), jnp.exp(-m_t))
            ms = jnp.mean(ht * ht, axis=0, keepdims=True)
            hn = (ht * lax.rsqrt(ms + NORM_EPS)).T
            og = jax.nn.sigmoid(o_ref[0, pl.ds(r0, L), hs].astype(F32))
            y_ref[0, pl.ds(r0, L), hs] = (og * (hn * ng_ref[:, hs])).astype(BF16)

            b_tot = b_row[:, L - 1:L]
            g = b_tot - b_row + i_row
            m_new = jnp.maximum(b_tot + m_prev, jnp.max(g, axis=1, keepdims=True))
            w_c = jnp.exp(b_tot + m_prev - m_new)
            w_s = jnp.exp(g - m_new)
            vw = jnp.concatenate([vt * w_s, ones_blk * w_s], axis=0).astype(BF16)
            ctn_ref[h] = w_c[:, :d] * ctn + _dot(vw, kc)
            m_ref[h] = jnp.broadcast_to(m_new, m_ref.shape[1:])
        return 0

    for c in range(nchunk):
        chunk(c, 0)


def _mlstm(qb, kb, vbt, ob, grow, gcol, norm_g):
    bn, s, w = qb.shape
    ts = TS_MLSTM
    nc = ts // MLSTM_L
    tok = pl.BlockSpec((1, ts, w), lambda b, t: (b, t, 0))
    return pl.pallas_call(
        _mlstm_kernel,
        grid=(bn, s // ts),
        in_specs=[
            tok, tok,
            pl.BlockSpec((1, nc, w, MLSTM_L), lambda b, t: (b, t, 0, 0)),
            tok,
            pl.BlockSpec((1, nc, 2 * B_HEADS, MLSTM_L), lambda b, t: (b, t, 0, 0)),
            pl.BlockSpec((1, ts, 2 * B_HEADS), lambda b, t: (b, t, 0)),
            _const_spec((1, w)),
        ],
        out_specs=tok,
        out_shape=jax.ShapeDtypeStruct((bn, s, w), BF16),
        scratch_shapes=[pltpu.VMEM((B_HEADS, AUG_ROWS, B_HEAD_DIM), F32),
                        pltpu.VMEM((B_HEADS, SUBLANES, MLSTM_L), F32)],
        compiler_params=pltpu.CompilerParams(
            dimension_semantics=("arbitrary", "arbitrary"), vmem_limit_bytes=VMEM_LIMIT_BYTES),
        name="mlstm_chunkwise",
    )(qb, kb, vbt, ob, grow, gcol, norm_g)


def _merge_ffn_kernel(x_ref, ya_ref, yb_ref, sga_ref, sgb_ref, wpa_ref, wpb_ref, wout_ref,
                      gffn_ref, wgu_ref, wdown_ref, gfin_ref, o_ref, u_ref, h_ref, act_ref, h2_ref):
    t = pl.program_id(1)
    nt = pl.num_programs(1) - 1
    nff = D_FF // FF_CHUNK

    def merge_and_norm():
        merged = (jax.nn.sigmoid(sga_ref[0].astype(F32)) * _dot(ya_ref[0], wpa_ref[...])
                  + jax.nn.sigmoid(sgb_ref[0].astype(F32)) * _dot(yb_ref[0], wpb_ref[...]))
        h = x_ref[0] + _dot(merged.astype(BF16), wout_ref[...])
        var = jnp.mean(h * h, axis=-1, keepdims=True)
        u_ref[...] = (h * lax.rsqrt(var + NORM_EPS) * gffn_ref[...]).astype(BF16)
        h_ref[...] = h

    def ffn_chunk(c):
        g = _dot(u_ref[...], wgu_ref[:, c * FF_CHUNK:(c + 1) * FF_CHUNK])
        up = _dot(u_ref[...], wgu_ref[:, D_FF + c * FF_CHUNK:D_FF + (c + 1) * FF_CHUNK])
        act_ref[:, c * FF_CHUNK:(c + 1) * FF_CHUNK] = (g * jax.nn.sigmoid(g) * up).astype(BF16)

    def down_proj():
        h2_ref[...] = h_ref[...] + _dot(act_ref[...], wdown_ref[...])

    def finish_previous_tile():
        h2 = h2_ref[...]
        var2 = jnp.mean(h2 * h2, axis=-1, keepdims=True)
        o_ref[0] = h2 * lax.rsqrt(var2 + NORM_EPS) * gfin_ref[...]

    @pl.when(t == 0)
    def _():
        o_ref[...] = jnp.zeros_like(o_ref)
        merge_and_norm()
        for c in range(nff):
            ffn_chunk(c)
        down_proj()

    @pl.when((t > 0) & (t < nt))
    def _():
        merge_and_norm()
        for c in range(nff):
            ffn_chunk(c)
            if c == FINISH_AFTER_CHUNK:
                finish_previous_tile()
        down_proj()

    @pl.when(t == nt)
    def _():
        finish_previous_tile()


def _merge_ffn(x, ya, yb, sga, sgb, wpa, wpb, wout, gffn, wgu, wdown, gfin):
    bn, s, d = x.shape
    tm = TM_OUT
    nt = s // tm
    tok = lambda w: pl.BlockSpec((1, tm, w), lambda b, t: (b, jnp.minimum(t, nt - 1), 0))
    return pl.pallas_call(
        _merge_ffn_kernel,
        grid=(bn, nt + 1),
        in_specs=[tok(d), tok(A_WIDTH), tok(B_WIDTH), tok(d), tok(d),
                  _const_spec(wpa.shape), _const_spec(wpb.shape), _const_spec(wout.shape),
                  _const_spec(gffn.shape), _const_spec(wgu.shape),
                  _const_spec(wdown.shape), _const_spec(gfin.shape)],
        out_specs=pl.BlockSpec((1, tm, d), lambda b, t: (b, jnp.maximum(t - 1, 0), 0)),
        out_shape=jax.ShapeDtypeStruct((bn, s, d), F32),
        scratch_shapes=[pltpu.VMEM((tm, d), BF16), pltpu.VMEM((tm, d), F32), pltpu.VMEM((tm, D_FF), BF16),
                        pltpu.VMEM((tm, d), F32)],
        compiler_params=pltpu.CompilerParams(
            dimension_semantics=("arbitrary", "arbitrary"), vmem_limit_bytes=VMEM_LIMIT_BYTES),
        name="merge_ffn",
    )(x, ya, yb, sga, sgb, wpa, wpb, wout, gffn, wgu, wdown, gfin)


def kernel(x, norm_mix_g, w_in, conv_w, b_igate, b_fgate, mlstm_norm_g, w_proj_a, w_proj_b,
           w_out, norm_ffn_g, w_gate_up, w_down, norm_final_g):
    bn, s, d = x.shape
    assert d == D_MODEL and s % TS_MLSTM == 0 and s % TM_IN == 0 and s % TM_OUT == 0
    assert norm_mix_g.shape[0] == 1, "single layer"
    w = w_in[0]
    o_qa, o_ka, o_va = 0, A_WIDTH, 2 * A_WIDTH
    o_qkb = 3 * A_WIDTH
    o_vb = o_qkb + 2 * B_WIDTH
    o_ob = o_vb + B_WIDTH
    o_i = o_ob + B_WIDTH
    o_f = o_i + B_HEADS
    o_ga = o_f + B_HEADS
    o_gb = o_ga + D_MODEL

    wqka = w[:, o_qa:o_va].astype(BF16)
    wqkb = w[:, o_qkb:o_vb].astype(BF16)
    wo = w[:, o_ob:o_i].astype(BF16)
    wg = w[:, o_ga:o_gb + D_MODEL].astype(BF16)
    w_fi = jnp.concatenate([w[:, o_f:o_f + B_HEADS], w[:, o_i:o_i + B_HEADS]], axis=1)
    wif = w_fi.astype(BF16)
    wift = w_fi.T.astype(BF16)
    wvt = jnp.concatenate([w[:, o_va:o_qkb], w[:, o_vb:o_ob]], axis=1).T.astype(BF16)
    bias_fi = jnp.concatenate([b_fgate[0], b_igate[0]]).astype(F32)
    bcol = bias_fi[None, :]
    brow = bias_fi[:, None]

    rope_t = _rope_tables(s)
    (qa, ka, kmean, vat, vbt, qb, kb, ob, sga, sgb, grow, gcol) = _in_proj(
        x, norm_mix_g, rope_t, conv_w[0], bcol, brow, wqka, wqkb, wo, wg, wif, wvt, wift)
    kmean = kmean.reshape(bn, s // MOBA_BLOCK, A_WIDTH)

    ya = _moba(qa, ka, vat, kmean)
    yb = _mlstm(qb, kb, vbt, ob, grow, gcol, mlstm_norm_g)

    wgu = w_gate_up[0].astype(BF16)
    return _merge_ffn(x, ya, yb, sga, sgb,
                      w_proj_a[0].astype(BF16), w_proj_b[0].astype(BF16), w_out[0].astype(BF16),
                      norm_ffn_g, wgu, w_down[0].astype(BF16), norm_final_g[None, :])
```

```python
import functools

import jax
import jax.numpy as jnp
import numpy as np
from jax import lax
from jax.experimental import pallas as pl
from jax.experimental.pallas import tpu as pltpu

F32 = jnp.float32
BF16 = jnp.bfloat16

D_MODEL = 1024
A_HEADS = 8
A_HEAD_DIM = 64
A_WIDTH = A_HEADS * A_HEAD_DIM
MOBA_BLOCK = 256
MOBA_TOPK = 3
ROPE_THETA = 10000.0
B_HEADS = 4
B_HEAD_DIM = 128
B_WIDTH = B_HEADS * B_HEAD_DIM
CONV_WIDTH = 4
D_FF = 2816
NORM_EPS = 1e-6

LANES = 128
SUBLANES = 8
BF16_SUBLANE_PACK = 16
VMEM_LIMIT_BYTES = 56 * 1024 * 1024

TM_IN = 512
MLSTM_L = 256
TS_MLSTM = 1024
TM_OUT = 512
FF_CHUNK = 256
IN_CHUNK = 256
AUG_ROWS = B_HEAD_DIM + BF16_SUBLANE_PACK
MOBA_HEADS_PER_STEP = 8
VA_ROWS = A_HEAD_DIM + BF16_SUBLANE_PACK
LOG2E = 1.4426950408889634
Q_SCALE = A_HEAD_DIM ** -0.5 * LOG2E
ROPE_TABS = 6
ROPE_ROWS = 64


def _dot(a, b):
    return jnp.dot(a, b, preferred_element_type=F32)


def _dot_nt(a, b):
    return lax.dot_general(a, b, (((1,), (1,)), ((), ())), preferred_element_type=F32)


def _split3(x):
    x1 = x.astype(BF16)
    r1 = x - x1.astype(F32)
    x2 = r1.astype(BF16)
    r2 = r1 - x2.astype(F32)
    return x1, x2, r2.astype(BF16)


def _const_spec(shape):
    nd = len(shape)
    return pl.BlockSpec(shape, lambda *_: (0,) * nd, pipeline_mode=pl.Buffered(1))


def _rope_table_kernel(inv_ref, tab_ref):
    s = tab_ref.shape[0]
    r = ROPE_ROWS
    row = lax.broadcasted_iota(jnp.int32, (r, LANES), 0).astype(F32)
    lane = lax.broadcasted_iota(jnp.int32, (r, LANES), 1)
    first_half = (lane % A_HEAD_DIM) < (A_HEAD_DIM // 2)
    ang_b = row * inv_ref[...]
    cb, sb = jnp.cos(ang_b), jnp.sin(ang_b)
    ang_a = (row * float(r)) * inv_ref[...]
    ca, sa = jnp.cos(ang_a), jnp.sin(ang_a)
    for a in range(s // r):
        cs = ca[a:a + 1, :] * cb - sa[a:a + 1, :] * sb
        sn = sa[a:a + 1, :] * cb + ca[a:a + 1, :] * sb
        parts = [cs, jnp.where(first_half, -sn, 0.0), jnp.where(first_half, 0.0, sn)]
        for i, p in enumerate(parts):
            tab_ref[a * r:(a + 1) * r, i * LANES:(i + 1) * LANES] = p
            tab_ref[a * r:(a + 1) * r, (i + 3) * LANES:(i + 4) * LANES] = p * Q_SCALE


def _rope_tables(s):
    half = A_HEAD_DIM // 2
    inv = ROPE_THETA ** (-jnp.arange(half, dtype=F32) / half)
    inv = jnp.tile(inv, LANES // half)[None, :]
    return pl.pallas_call(
        _rope_table_kernel,
        out_shape=jax.ShapeDtypeStruct((s, ROPE_TABS * LANES), F32),
        name="rope_tables",
    )(inv)


def _log_sigmoid(x):
    return jnp.minimum(x, 0.0) - jnp.log(1.0 + jnp.exp(-jnp.abs(x)))


def _inproj_kernel(x_ref, g_ref, rope_ref, convw_ref, bcol_ref, brow_ref,
                   wqka_ref, wqkb_ref, wo_ref, wg_ref, wif_ref, wvt_ref, wift_ref,
                   qa_ref, ka_ref, kmean_ref, vat_ref, vbt_ref, qb_ref, kb_ref, ob_ref,
                   sga_ref, sgb_ref, grow_ref, gcol_ref,
                   u_ref, zs_ref, ys_ref):
    tm = x_ref.shape[1]
    nchunk = tm // MOBA_BLOCK
    t_idx = pl.program_id(1)

    xf = x_ref[0]
    var = jnp.mean(xf * xf, axis=-1, keepdims=True)
    u_ref[...] = (xf * lax.rsqrt(var + NORM_EPS) * g_ref[...]).astype(BF16)
    u = u_ref[...]

    half = A_HEAD_DIM // 2

    NW = IN_CHUNK

    def rope_store(z, tab0, out_ref, col0, with_mean):
        c, s_lo, s_hi = [rope_ref[:, (tab0 + i) * LANES:(tab0 + i + 1) * LANES] for i in range(3)]
        for lt in range(NW // LANES):
            zt = z[:, lt * LANES:(lt + 1) * LANES]
            rt = zt * c + pltpu.roll(zt, LANES - half, 1) * s_lo + pltpu.roll(zt, half, 1) * s_hi
            ls = slice(col0 + lt * LANES, col0 + (lt + 1) * LANES)
            out_ref[0, :, ls] = rt.astype(BF16)
            if with_mean:
                for cb in range(nchunk):
                    kmean_ref[0, 0, cb:cb + 1, ls] = jnp.mean(rt[cb * MOBA_BLOCK:(cb + 1) * MOBA_BLOCK],
                                                             axis=0, keepdims=True)

    def q_piece(p):
        rope_store(_dot(u_ref[...], wqka_ref[:, p * NW:(p + 1) * NW]), 3, qa_ref, p * NW, False)

    def k_piece(p):
        rope_store(_dot(u_ref[...], wqka_ref[:, A_WIDTH + p * NW:A_WIDTH + (p + 1) * NW]), 0, ka_ref, p * NW, True)

    ones_row = lax.broadcasted_iota(jnp.int32, (BF16_SUBLANE_PACK, MOBA_BLOCK), 0) == 0
    ones_blk = jnp.where(ones_row, 1.0, 0.0).astype(BF16)
    heads_per_piece = NW // A_HEAD_DIM

    def va_piece(p):
        vat = _dot_nt(wvt_ref[p * NW:(p + 1) * NW, :], u_ref[...])
        for c in range(nchunk):
            cs = slice(c * MOBA_BLOCK, (c + 1) * MOBA_BLOCK)
            for hh in range(heads_per_piece):
                h = p * heads_per_piece + hh
                piece = vat[hh * A_HEAD_DIM:(hh + 1) * A_HEAD_DIM, cs].astype(BF16)
                vat_ref[0, c, h * VA_ROWS:h * VA_ROWS + A_HEAD_DIM, :] = piece
                vat_ref[0, c, h * VA_ROWS + A_HEAD_DIM:(h + 1) * VA_ROWS, :] = ones_blk

    def vb_piece(p):
        vbt = _dot_nt(wvt_ref[A_WIDTH + p * NW:A_WIDTH + (p + 1) * NW, :], u_ref[...])
        for c in range(nchunk):
            vbt_ref[0, c, p * NW:(p + 1) * NW, :] = vbt[:, c * MOBA_BLOCK:(c + 1) * MOBA_BLOCK].astype(BF16)

    @pl.when(t_idx == 0)
    def _():
        zs_ref[:, :SUBLANES, :] = jnp.zeros((zs_ref.shape[0], SUBLANES, LANES), F32)

    def conv_piece(p):
        ps = slice(p * NW, (p + 1) * NW)
        z = _dot(u_ref[...], wqkb_ref[:, ps])
        ngrp = tm // SUBLANES
        for lt in range(NW // LANES):
            zl = zs_ref.at[p * (NW // LANES) + lt]
            yl = ys_ref.at[lt]
            col = p * NW + lt * LANES
            zl[SUBLANES:, :] = z[:, lt * LANES:(lt + 1) * LANES]
            taps = {}
            for r in range(SUBLANES):
                conv = None
                for j in range(CONV_WIDTH):
                    off = SUBLANES - (CONV_WIDTH - 1) + r + j
                    if off not in taps:
                        taps[off] = zl[pl.ds(off, ngrp, stride=SUBLANES), :]
                    term = taps[off] * convw_ref[j:j + 1, col:col + LANES]
                    conv = term if conv is None else conv + term
                act = conv * jax.nn.sigmoid(conv)
                if p >= B_WIDTH // NW:
                    act = act * (B_HEAD_DIM ** -0.5)
                yl[pl.ds(r, ngrp, stride=SUBLANES), :] = act
            zl[:SUBLANES, :] = zl[tm:, :]
            out_ref, c0 = (qb_ref, col) if p < B_WIDTH // NW else (kb_ref, col - B_WIDTH)
            out_ref[0, :, c0:c0 + LANES] = yl[...].astype(BF16)

    def o_piece(p):
        ps = slice(p * NW, (p + 1) * NW)
        ob_ref[0, :, ps] = _dot(u_ref[...], wo_ref[:, ps]).astype(BF16)

    def gate_piece(p):
        ps = slice(p * NW, (p + 1) * NW)
        sga_ref[0, :, ps] = _dot(u_ref[...], wg_ref[:, ps]).astype(BF16)
        sgb_ref[0, :, ps] = _dot(u_ref[...], wg_ref[:, D_MODEL + p * NW:D_MODEL + (p + 1) * NW]).astype(BF16)

    assert A_WIDTH // NW == 2 and B_WIDTH // NW == 2 and D_MODEL // NW == 4
    q_piece(0); va_piece(0)
    q_piece(1); va_piece(1)
    k_piece(0); vb_piece(0)
    k_piece(1); vb_piece(1)
    conv_piece(0); gate_piece(0)
    conv_piece(1); gate_piece(1)
    conv_piece(2); gate_piece(2); o_piece(0)
    conv_piece(3); gate_piece(3); o_piece(1)

    L = MLSTM_L
    zc = _dot(u, wif_ref[...]) + bcol_ref[...]
    lane8 = lax.broadcasted_iota(jnp.int32, zc.shape, 1)
    pre_c = jnp.where(lane8 < B_HEADS, _log_sigmoid(zc), zc)
    zr = _dot_nt(wift_ref[...], u) + brow_ref[...]
    row8 = lax.broadcasted_iota(jnp.int32, zr.shape, 0)
    pre_r = jnp.where(row8 < B_HEADS, _log_sigmoid(zr), zr)
    ti = lax.broadcasted_iota(jnp.int32, (L, L), 0)
    tj = lax.broadcasted_iota(jnp.int32, (L, L), 1)
    tril = jnp.where(tj <= ti, 1.0, 0.0).astype(BF16)
    triu = jnp.where(ti <= tj, 1.0, 0.0).astype(BF16)
    lane8_c = lax.broadcasted_iota(jnp.int32, (L, 2 * B_HEADS), 1)
    row8_c = lax.broadcasted_iota(jnp.int32, (2 * B_HEADS, L), 0)
    for c in range(tm // L):
        pc = pre_c[c * L:(c + 1) * L, :]
        p1, p2, p3 = _split3(pc)
        cum = _dot(tril, p1) + _dot(tril, p2) + _dot(tril, p3)
        gcol_ref[0, c * L:(c + 1) * L, :] = jnp.where(lane8_c < B_HEADS, cum, pc)
        pr = pre_r[:, c * L:(c + 1) * L]
        r1, r2, r3 = _split3(pr)
        cumr = _dot(r1, triu) + _dot(r2, triu) + _dot(r3, triu)
        grow_ref[0, c] = jnp.where(row8_c < B_HEADS, cumr, pr)


def _in_proj(x, norm_g, rope_t, conv_w, bcol, brow, wqka, wqkb, wo, wg, wif, wvt, wift):
    bn, s, d = x.shape
    tm = TM_IN
    nt = s // tm
    nc = tm // MOBA_BLOCK
    nblk = s // MOBA_BLOCK
    tok = lambda w: pl.BlockSpec((1, tm, w), lambda b, t: (b, t, 0))
    blk4 = lambda r: pl.BlockSpec((1, nc, r, MOBA_BLOCK), lambda b, t: (b, t, 0, 0))
    in_specs = [
        tok(d),
        _const_spec((1, d)),
        pl.BlockSpec((tm, ROPE_TABS * LANES), lambda b, t: (t, 0)),
        _const_spec(conv_w.shape), _const_spec(bcol.shape), _const_spec(brow.shape),
        _const_spec(wqka.shape), _const_spec(wqkb.shape), _const_spec(wo.shape),
        _const_spec(wg.shape), _const_spec(wif.shape), _const_spec(wvt.shape), _const_spec(wift.shape),
    ]
    out_shape = (
        jax.ShapeDtypeStruct((bn, s, A_WIDTH), BF16),
        jax.ShapeDtypeStruct((bn, s, A_WIDTH), BF16),
        jax.ShapeDtypeStruct((bn, nt, nc, A_WIDTH), F32),
        jax.ShapeDtypeStruct((bn, nblk, A_HEADS * VA_ROWS, MOBA_BLOCK), BF16),
        jax.ShapeDtypeStruct((bn, nblk, B_WIDTH, MOBA_BLOCK), BF16),
        jax.ShapeDtypeStruct((bn, s, B_WIDTH), BF16),
        jax.ShapeDtypeStruct((bn, s, B_WIDTH), BF16),
        jax.ShapeDtypeStruct((bn, s, B_WIDTH), BF16),
        jax.ShapeDtypeStruct((bn, s, D_MODEL), BF16),
        jax.ShapeDtypeStruct((bn, s, D_MODEL), BF16),
        jax.ShapeDtypeStruct((bn, nblk, 2 * B_HEADS, MOBA_BLOCK), F32),
        jax.ShapeDtypeStruct((bn, s, 2 * B_HEADS), F32),
    )
    out_specs = (
        tok(A_WIDTH), tok(A_WIDTH),
        pl.BlockSpec((1, 1, nc, A_WIDTH), lambda b, t: (b, t, 0, 0)),
        blk4(A_HEADS * VA_ROWS), blk4(B_WIDTH),
        tok(B_WIDTH), tok(B_WIDTH), tok(B_WIDTH),
        tok(D_MODEL), tok(D_MODEL),
        blk4(2 * B_HEADS),
        tok(2 * B_HEADS),
    )
    return pl.pallas_call(
        _inproj_kernel,
        grid=(bn, nt),
        in_specs=in_specs,
        out_specs=out_specs,
        out_shape=out_shape,
        scratch_shapes=[pltpu.VMEM((tm, d), BF16),
                        pltpu.VMEM((2 * B_WIDTH // LANES, tm + SUBLANES, LANES), F32),
                        pltpu.VMEM((IN_CHUNK // LANES, tm, LANES), F32)],
        compiler_params=pltpu.CompilerParams(
            dimension_semantics=("arbitrary", "arbitrary"), vmem_limit_bytes=VMEM_LIMIT_BYTES),
        name="in_proj",
    )(x, norm_g, rope_t, conv_w, bcol, brow, wqka, wqkb, wo, wg, wif, wvt, wift)


def _moba_kernel(q_ref, k_ref, vt_ref, kmean_ref, o_ref, sel_ref, acc_ref, mx_ref, m_ref, s_ref):
    blk = MOBA_BLOCK
    nblk = kmean_ref.shape[1]
    hg = q_ref.shape[2] // A_HEAD_DIM
    i = pl.program_id(2)

    def pair_lanes(h):
        return slice((h // 2) * LANES, (h // 2 + 1) * LANES)

    def head_rows(h):
        return slice(h * VA_ROWS, (h + 1) * VA_ROWS)

    def group_max(st):
        return jnp.max(st.reshape(blk // SUBLANES, SUBLANES, blk), axis=0)

    @pl.when(i == 0)
    def _():
        m_ref[...] = jnp.zeros_like(m_ref)
        o_ref[...] = jnp.zeros_like(o_ref)

    for h in range(hg):
        acc_ref[h] = jnp.zeros(acc_ref.shape[1:], F32)
    m_prev = [m_ref[h, 0:1, :] for h in range(hg)]

    def value_tiles(h, ts):
        part = None
        for t in ts:
            p = jnp.exp2(s_ref[h, t] - m_prev[h]).astype(BF16)
            d = _dot(vt_ref[0, t, head_rows(h), :], p)
            part = d if part is None else part + d
        acc_ref[h] += part

    @pl.when(i < nblk)
    def _():
        lane = lax.broadcasted_iota(jnp.int32, (blk, LANES), 1)
        lo_half = lane < A_HEAD_DIM
        qh = []
        for h in range(hg):
            qp = q_ref[0, :, pair_lanes(h)]
            qh.append(jnp.where(lo_half if h % 2 == 0 else jnp.logical_not(lo_half), qp, jnp.zeros_like(qp)))

        km = kmean_ref[0]
        lane_w = lax.broadcasted_iota(jnp.int32, km.shape, 1)
        kbd = jnp.concatenate(
            [jnp.where((lane_w >= h * A_HEAD_DIM) & (lane_w < (h + 1) * A_HEAD_DIM), km, 0.0) for h in range(hg)],
            axis=0)
        kbd_hi = kbd.astype(BF16)
        kbd_lo = (kbd - kbd_hi.astype(F32)).astype(BF16)
        q_all = q_ref[0]
        gate = (_dot_nt(kbd_hi, q_all) + _dot_nt(kbd_lo, q_all)).reshape(hg, nblk, blk)
        bidx = lax.broadcasted_iota(jnp.int32, (hg, nblk, blk), 1)
        past = bidx < i
        gate = jnp.where(past, gate, -jnp.inf)
        beats = []
        for jp in range(nblk):
            gj = gate[:, jp:jp + 1, :]
            beats.append(jnp.where(bidx > jp, jnp.where(gj >= gate, 1.0, 0.0), jnp.where(gj > gate, 1.0, 0.0)))
        while len(beats) > 1:
            beats = [a + b for a, b in zip(beats[::2], beats[1::2])]
        sel_ref[...] = jnp.where(past & (beats[0] < MOBA_TOPK), 1.0, 0.0)

        key_i = lax.broadcasted_iota(jnp.int32, (blk, blk), 0)
        qry_i = lax.broadcasted_iota(jnp.int32, (blk, blk), 1)
        causal = key_i <= qry_i

        def score_tile(h, t, m, own):
            kt = k_ref[0, pl.ds(pl.multiple_of(t * blk, blk), blk), pair_lanes(h)]
            keep = causal if own else sel_ref[h, pl.ds(t, 1), :] > 0.5
            st = jnp.where(keep, _dot_nt(kt, qh[h]), -jnp.inf)
            s_ref[h, t] = st
            return jnp.maximum(m, group_max(st))

        def key_blocks(ts, mx):
            out = []
            for h in range(hg):
                value_tiles(h, ts)
                m = mx[h]
                for t in ts:
                    m = score_tile(h, t, m, False)
                out.append(m)
            return tuple(out)

        def key_blocks_via_ref(ts):
            out = key_blocks(ts, tuple(mx_ref[h] for h in range(hg)))
            for h in range(hg):
                mx_ref[h] = out[h]

        mx0 = tuple(score_tile(h, i, jnp.full((SUBLANES, blk), -jnp.inf, F32), True) for h in range(hg))
        nquad = lax.shift_right_logical(i, 2)
        mx = lax.fori_loop(
            0, nquad,
            lambda t4, mx: key_blocks([4 * t4 + 2, 4 * t4 + 3], key_blocks([4 * t4, 4 * t4 + 1], mx)), mx0)
        for h in range(hg):
            mx_ref[h] = mx[h]

        @pl.when(lax.rem(lax.shift_right_logical(i, 1), 2) == 1)
        def _():
            key_blocks_via_ref([4 * nquad, 4 * nquad + 1])

        @pl.when(lax.rem(i, 2) == 1)
        def _():
            key_blocks_via_ref([i - 1])

        for h in range(hg):
            m_ref[h] = jnp.broadcast_to(jnp.max(mx_ref[h], axis=0, keepdims=True), m_ref.shape[1:])

    @pl.when(i == nblk)
    def _():
        def body(t2, _):
            for h in range(hg):
                value_tiles(h, [2 * t2, 2 * t2 + 1])
            return 0

        lax.fori_loop(0, nblk // 2, body, 0)

    @pl.when(i > 0)
    def _():
        outs = []
        for h in range(hg):
            acc = acc_ref[h]
            outs.append(acc[:A_HEAD_DIM] / acc[A_HEAD_DIM:A_HEAD_DIM + 1])
        o_ref[0] = jnp.concatenate(outs, axis=0).T.astype(BF16)


def _moba(qa, ka, vat, kmean):
    bn, s, _ = qa.shape
    blk = MOBA_BLOCK
    nblk = s // blk
    hg = MOBA_HEADS_PER_STEP
    ngroup = A_HEADS // hg
    gw = hg * A_HEAD_DIM
    return pl.pallas_call(
        _moba_kernel,
        grid=(bn, ngroup, nblk + 1),
        in_specs=[
            pl.BlockSpec((1, blk, gw), lambda b, g, i: (b, jnp.minimum(i, nblk - 1), g)),
            pl.BlockSpec((1, s, gw), lambda b, g, i: (b, 0, g)),
            pl.BlockSpec((1, nblk, hg * VA_ROWS, blk), lambda b, g, i: (b, 0, g, 0)),
            pl.BlockSpec((1, nblk, gw), lambda b, g, i: (b, 0, g)),
        ],
        out_specs=pl.BlockSpec((1, blk, gw), lambda b, g, i: (b, jnp.maximum(i - 1, 0), g)),
        out_shape=jax.ShapeDtypeStruct((bn, s, A_WIDTH), BF16),
        scratch_shapes=[pltpu.VMEM((hg, nblk, blk), F32),
                        pltpu.VMEM((hg, VA_ROWS, blk), F32),
                        pltpu.VMEM((hg, SUBLANES, blk), F32),
                        pltpu.VMEM((hg, SUBLANES, blk), F32),
                        pltpu.VMEM((hg, nblk, blk, blk), F32)],
        compiler_params=pltpu.CompilerParams(
            dimension_semantics=("arbitrary", "arbitrary", "arbitrary"), vmem_limit_bytes=VMEM_LIMIT_BYTES),
        name="moba_attention",
    )(qa, ka, vat, kmean)


def _mlstm_kernel(q_ref, k_ref, vt_ref, o_ref, grow_ref, gcol_ref, ng_ref, y_ref, ctn_ref, m_ref):
    L = MLSTM_L
    d = B_HEAD_DIM
    nchunk = q_ref.shape[1] // L

    @pl.when(pl.program_id(1) == 0)
    def _():
        ctn_ref[...] = jnp.zeros_like(ctn_ref)
        m_ref[...] = jnp.zeros_like(m_ref)

    s_i = lax.broadcasted_iota(jnp.int32, (L, L), 0)
    t_i = lax.broadcasted_iota(jnp.int32, (L, L), 1)
    causal = s_i <= t_i
    aug_row = lax.broadcasted_iota(jnp.int32, (BF16_SUBLANE_PACK, L), 0)
    ones_blk = jnp.where(aug_row == 0, 1.0, 0.0)

    def chunk(c, _):
        r0 = c * L
        grow = grow_ref[0, c]
        gcol = gcol_ref[0, pl.ds(r0, L), :]
        for h in range(B_HEADS):
            hs = slice(h * d, (h + 1) * d)
            qc = q_ref[0, pl.ds(r0, L), hs]
            kc = k_ref[0, pl.ds(r0, L), hs]
            vt = vt_ref[0, c, hs, :].astype(F32)
            b_row = grow[h:h + 1, :]
            i_row = grow[B_HEADS + h:B_HEADS + h + 1, :]
            r_col = gcol[:, B_HEADS + h:B_HEADS + h + 1] - gcol[:, h:h + 1]
            m_prev = m_ref[h, 0:1, :]

            dm = jnp.where(causal, r_col + b_row, -jnp.inf)
            m_inter = b_row + m_prev
            m_t = jnp.maximum(m_inter, jnp.max(dm, axis=0, keepdims=True))
            pw = (_dot_nt(kc, qc) * jnp.exp(dm - m_t)).astype(BF16)
            w_inter = jnp.exp(m_inter - m_t)
            vt_aug = jnp.concatenate([vt, ones_blk], axis=0).astype(BF16)
            ctn = ctn_ref[h]
            tot = _dot(vt_aug, pw) + w_inter * _dot_nt(ctn.astype(BF16), qc)
            den = tot[d:d + 1, :]
            ht = tot[:d, :] / jnp.maximum(jnp.abs(den), jnp.exp(-m_t))
            ms = jnp.mean(ht * ht, axis=0, keepdims=True)
            hn = (ht * lax.rsqrt(ms + NORM_EPS)).T
            og = jax.nn.sigmoid(o_ref[0, pl.ds(r0, L), hs].astype(F32))
            y_ref[0, pl.ds(r0, L), hs] = (og * (hn * ng_ref[:, hs])).astype(BF16)

            b_tot = b_row[:, L - 1:L]
            g = b_tot - b_row + i_row
            m_new = jnp.maximum(b_tot + m_prev, jnp.max(g, axis=1, keepdims=True))
            w_c = jnp.exp(b_tot + m_prev - m_new)
            w_s = jnp.exp(g - m_new)
            vw = jnp.concatenate([vt * w_s, ones_blk * w_s], axis=0).astype(BF16)
            ctn_ref[h] = w_c[:, :d] * ctn + _dot(vw, kc)
            m_ref[h] = jnp.broadcast_to(m_new, m_ref.shape[1:])
        return 0

    for c in range(nchunk):
        chunk(c, 0)


def _mlstm(qb, kb, vbt, ob, grow, gcol, norm_g):
    bn, s, w = qb.shape
    ts = TS_MLSTM
    nc = ts // MLSTM_L
    tok = pl.BlockSpec((1, ts, w), lambda b, t: (b, t, 0))
    return pl.pallas_call(
        _mlstm_kernel,
        grid=(bn, s // ts),
        in_specs=[
            tok, tok,
            pl.BlockSpec((1, nc, w, MLSTM_L), lambda b, t: (b, t, 0, 0)),
            tok,
            pl.BlockSpec((1, nc, 2 * B_HEADS, MLSTM_L), lambda b, t: (b, t, 0, 0)),
            pl.BlockSpec((1, ts, 2 * B_HEADS), lambda b, t: (b, t, 0)),
            _const_spec((1, w)),
        ],
        out_specs=tok,
        out_shape=jax.ShapeDtypeStruct((bn, s, w), BF16),
        scratch_shapes=[pltpu.VMEM((B_HEADS, AUG_ROWS, B_HEAD_DIM), F32),
                        pltpu.VMEM((B_HEADS, SUBLANES, MLSTM_L), F32)],
        compiler_params=pltpu.CompilerParams(
            dimension_semantics=("arbitrary", "arbitrary"), vmem_limit_bytes=VMEM_LIMIT_BYTES),
        name="mlstm_chunkwise",
    )(qb, kb, vbt, ob, grow, gcol, norm_g)


def _merge_ffn_kernel(x_ref, ya_ref, yb_ref, sga_ref, sgb_ref, wpa_ref, wpb_ref, wout_ref,
                      gffn_ref, wgu_ref, wdown_ref, gfin_ref, o_ref, u_ref, h_ref, act_ref):
    merged = (jax.nn.sigmoid(sga_ref[0].astype(F32)) * _dot(ya_ref[0], wpa_ref[...])
              + jax.nn.sigmoid(sgb_ref[0].astype(F32)) * _dot(yb_ref[0], wpb_ref[...]))
    h = x_ref[0] + _dot(merged.astype(BF16), wout_ref[...])
    var = jnp.mean(h * h, axis=-1, keepdims=True)
    u_ref[...] = (h * lax.rsqrt(var + NORM_EPS) * gffn_ref[...]).astype(BF16)
    h_ref[...] = h

    for c in range(D_FF // FF_CHUNK):
        g = _dot(u_ref[...], wgu_ref[:, c * FF_CHUNK:(c + 1) * FF_CHUNK])
        up = _dot(u_ref[...], wgu_ref[:, D_FF + c * FF_CHUNK:D_FF + (c + 1) * FF_CHUNK])
        act_ref[:, c * FF_CHUNK:(c + 1) * FF_CHUNK] = (g * jax.nn.sigmoid(g) * up).astype(BF16)

    h2 = h_ref[...] + _dot(act_ref[...], wdown_ref[...])
    var2 = jnp.mean(h2 * h2, axis=-1, keepdims=True)
    o_ref[0] = h2 * lax.rsqrt(var2 + NORM_EPS) * gfin_ref[...]


def _merge_ffn(x, ya, yb, sga, sgb, wpa, wpb, wout, gffn, wgu, wdown, gfin):
    bn, s, d = x.shape
    tm = TM_OUT
    tok = lambda w: pl.BlockSpec((1, tm, w), lambda b, t: (b, t, 0))
    return pl.pallas_call(
        _merge_ffn_kernel,
        grid=(bn, s // tm),
        in_specs=[tok(d), tok(A_WIDTH), tok(B_WIDTH), tok(d), tok(d),
                  _const_spec(wpa.shape), _const_spec(wpb.shape), _const_spec(wout.shape),
                  _const_spec(gffn.shape), _const_spec(wgu.shape),
                  _const_spec(wdown.shape), _const_spec(gfin.shape)],
        out_specs=tok(d),
        out_shape=jax.ShapeDtypeStruct((bn, s, d), F32),
        scratch_shapes=[pltpu.VMEM((tm, d), BF16), pltpu.VMEM((tm, d), F32), pltpu.VMEM((tm, D_FF), BF16)],
        compiler_params=pltpu.CompilerParams(
            dimension_semantics=("arbitrary", "arbitrary"), vmem_limit_bytes=VMEM_LIMIT_BYTES),
        name="merge_ffn",
    )(x, ya, yb, sga, sgb, wpa, wpb, wout, gffn, wgu, wdown, gfin)


def kernel(x, norm_mix_g, w_in, conv_w, b_igate, b_fgate, mlstm_norm_g, w_proj_a, w_proj_b,
           w_out, norm_ffn_g, w_gate_up, w_down, norm_final_g):
    bn, s, d = x.shape
    assert d == D_MODEL and s % TS_MLSTM == 0 and s % TM_IN == 0 and s % TM_OUT == 0
    assert norm_mix_g.shape[0] == 1, "single layer"
    w = w_in[0]
    o_qa, o_ka, o_va = 0, A_WIDTH, 2 * A_WIDTH
    o_qkb = 3 * A_WIDTH
    o_vb = o_qkb + 2 * B_WIDTH
    o_ob = o_vb + B_WIDTH
    o_i = o_ob + B_WIDTH
    o_f = o_i + B_HEADS
    o_ga = o_f + B_HEADS
    o_gb = o_ga + D_MODEL

    wqka = w[:, o_qa:o_va].astype(BF16)
    wqkb = w[:, o_qkb:o_vb].astype(BF16)
    wo = w[:, o_ob:o_i].astype(BF16)
    wg = w[:, o_ga:o_gb + D_MODEL].astype(BF16)
    w_fi = jnp.concatenate([w[:, o_f:o_f + B_HEADS], w[:, o_i:o_i + B_HEADS]], axis=1)
    wif = w_fi.astype(BF16)
    wift = w_fi.T.astype(BF16)
    wvt = jnp.concatenate([w[:, o_va:o_qkb], w[:, o_vb:o_ob]], axis=1).T.astype(BF16)
    bias_fi = jnp.concatenate([b_fgate[0], b_igate[0]]).astype(F32)
    bcol = bias_fi[None, :]
    brow = bias_fi[:, None]

    rope_t = _rope_tables(s)
    (qa, ka, kmean, vat, vbt, qb, kb, ob, sga, sgb, grow, gcol) = _in_proj(
        x, norm_mix_g, rope_t, conv_w[0], bcol, brow, wqka, wqkb, wo, wg, wif, wvt, wift)
    kmean = kmean.reshape(bn, s // MOBA_BLOCK, A_WIDTH)

    ya = _moba(qa, ka, vat, kmean)
    yb = _mlstm(qb, kb, vbt, ob, grow, gcol, mlstm_norm_g)

    wgu = w_gate_up[0].astype(BF16)
    return _merge_ffn(x, ya, yb, sga, sgb,
                      w_proj_a[0].astype(BF16), w_proj_b[0].astype(BF16), w_out[0].astype(BF16),
                      norm_ffn_g, wgu, w_down[0].astype(BF16), norm_final_g[None, :])
```
